```python
import jax, jax.numpy as jnp
from jax import lax
import numpy as np

D_MODEL = 1024
BATCH = 16
SEQ = 2048
DEPTH = 1

SSM_EXPAND = 2
SSM_D_INNER = SSM_EXPAND * D_MODEL
SSM_HEAD_DIM = 64
SSM_HEADS = SSM_D_INNER // SSM_HEAD_DIM
SSM_GROUPS = 4
SSM_STATE = 128
SSM_CONV = 4
SSM_CHUNK = 128
SSM_CONV_DIM = SSM_D_INNER + 2 * SSM_GROUPS * SSM_STATE
ATT_HEADS = 16
ATT_HEAD_DIM = 64
ATT_WIDTH = ATT_HEADS * ATT_HEAD_DIM
IDX_HEADS = 8
IDX_HEAD_DIM = 64
TOPK_MAX = 256
Q_BLOCK = 128
NEG_INF = -1e30
MEM_LEN = 256
X_HEADS = 4
X_HEAD_DIM = D_MODEL // X_HEADS
MOE_GROUPS = 4
MOE_EXPERTS_PER_GROUP = 8
MOE_EXPERTS = MOE_GROUPS * MOE_EXPERTS_PER_GROUP
MOE_TOPK = 2
MOE_HIDDEN = 512
EPS = 1e-6

IN_SPLIT_SIZES = (SSM_D_INNER, SSM_CONV_DIM, SSM_HEADS,
                  ATT_WIDTH, ATT_HEAD_DIM, ATT_HEAD_DIM,
                  IDX_HEADS * IDX_HEAD_DIM, IDX_HEAD_DIM, IDX_HEADS,
                  D_MODEL, D_MODEL)
D_IN_PROJ = sum(IN_SPLIT_SIZES)

kernel_name = "hybrid_ssd_dsa_hmoe_block"


def _split_points():
    pts, acc = [], 0
    for s in IN_SPLIT_SIZES[:-1]:
        acc += s
        pts.append(acc)
    return pts


def rms_norm(x, g):
    xf = x.astype(jnp.float32)
    y = xf * lax.rsqrt(jnp.mean(xf * xf, axis=-1, keepdims=True) + EPS)
    return (y * g.astype(jnp.float32)).astype(x.dtype)


def causal_dwconv(x, w, bias):
    k = w.shape[1]
    xp = jnp.pad(x, ((0, 0), (k - 1, 0), (0, 0)))
    rhs = jnp.transpose(w)[:, None, :].astype(x.dtype)
    y = lax.conv_general_dilated(xp, rhs, window_strides=(1,), padding='VALID',
                                 dimension_numbers=('NWC', 'WIO', 'NWC'),
                                 feature_group_count=x.shape[-1])
    return y + bias.astype(x.dtype)


def ssd_chunked(xh, dt, a, bm, cm):
    b, l, h, p = xh.shape
    g, n = bm.shape[2], bm.shape[3]
    kh = h // g
    c = l // SSM_CHUNK
    q = SSM_CHUNK
    x = xh.reshape(b, c, q, g, kh, p)
    dt = dt.reshape(b, c, q, g, kh)
    bc = bm.reshape(b, c, q, g, n)
    cc = cm.reshape(b, c, q, g, n)
    a_cs = jnp.cumsum(dt * a.reshape(g, kh), axis=2)
    xdt = x * dt[..., None]
    seg = a_cs[:, :, :, None] - a_cs[:, :, None, :]
    causal = jnp.tril(jnp.ones((q, q), dtype=bool))[:, :, None, None]
    lmat = jnp.exp(jnp.where(causal, seg, -jnp.inf))
    cb = jnp.einsum('bctgn,bcsgn->bctsg', cc, bc)
    y_diag = jnp.einsum('bctsg,bctsgk,bcsgkp->bctgkp', cb, lmat, xdt)
    decay_states = jnp.exp(a_cs[:, :, -1:] - a_cs)
    states = jnp.einsum('bcsgn,bcsgk,bcsgkp->bcgkpn', bc, decay_states, xdt)
    chunk_decay = jnp.exp(a_cs[:, :, -1])

    def step(hprev, inp):
        st, dec = inp
        return hprev * dec[..., None, None] + st, hprev

    h0 = jnp.zeros((b, g, kh, p, n), jnp.float32)
    _, states_in = lax.scan(step, h0, (jnp.moveaxis(states, 1, 0), jnp.moveaxis(chunk_decay, 1, 0)))
    states_in = jnp.moveaxis(states_in, 0, 1)
    y_off = jnp.einsum('bctgn,bcgkpn,bctgk->bctgkp', cc, states_in, jnp.exp(a_cs))
    return (y_diag + y_off).reshape(b, l, h, p)


def ssm_branch(z, xbc, dt_raw, conv_w, conv_b, dt_bias, a_log, d_skip, ssm_norm):
    b, l, _ = z.shape
    xbc = jax.nn.silu(causal_dwconv(xbc, conv_w, conv_b))
    xs, bm, cm = jnp.split(xbc, [SSM_D_INNER, SSM_D_INNER + SSM_GROUPS * SSM_STATE], axis=-1)
    xh = xs.reshape(b, l, SSM_HEADS, SSM_HEAD_DIM).astype(jnp.float32)
    bm = bm.reshape(b, l, SSM_GROUPS, SSM_STATE).astype(jnp.float32)
    cm = cm.reshape(b, l, SSM_GROUPS, SSM_STATE).astype(jnp.float32)
    dt = jax.nn.softplus(dt_raw.astype(jnp.float32) + dt_bias.astype(jnp.float32))
    a = -jnp.exp(a_log.astype(jnp.float32))
    y = ssd_chunked(xh, dt, a, bm, cm) + d_skip.astype(jnp.float32)[:, None] * xh
    y = y.reshape(b, l, SSM_D_INNER).astype(z.dtype)
    return rms_norm(y * jax.nn.silu(z), ssm_norm)


def dsa_branch(q, k, v, qi, ki, wi):
    b, l, _ = q.shape
    topk = min(TOPK_MAX, l // 4)
    nb = l // Q_BLOCK
    q = q.reshape(b, nb, Q_BLOCK, ATT_HEADS, ATT_HEAD_DIM)
    qi = qi.reshape(b, nb, Q_BLOCK, IDX_HEADS, IDX_HEAD_DIM).astype(jnp.float32)
    wi = wi.reshape(b, nb, Q_BLOCK, IDX_HEADS).astype(jnp.float32) * (IDX_HEADS ** -0.5 * IDX_HEAD_DIM ** -0.5)
    kif = ki.astype(jnp.float32)
    slopes = jnp.exp2(-8.0 * jnp.arange(1, ATT_HEADS + 1, dtype=jnp.float32) / ATT_HEADS)
    spos = jnp.arange(l)
    gather = jax.vmap(lambda arr, ix: arr[ix])

    def block(args):
        blk, qb, qib, wib = args
        tpos = blk * Q_BLOCK + jnp.arange(Q_BLOCK)
        rel = jax.nn.relu(jnp.einsum('bthd,bsd->bths', qib, kif))
        iscore = jnp.einsum('bths,bth->bts', rel, wib)
        iscore = jnp.where(spos[None, None, :] <= tpos[None, :, None], iscore, NEG_INF)
        _, idx = lax.top_k(iscore, topk)
        ksel = gather(k, idx)
        vsel = gather(v, idx)
        s = jnp.einsum('bthd,btkd->bthk', qb, ksel).astype(jnp.float32) * (ATT_HEAD_DIM ** -0.5)
        dist = (tpos[None, :, None] - idx).astype(jnp.float32)
        s = s - slopes[:, None] * dist[:, :, None, :]
        valid = (idx <= tpos[None, :, None])[:, :, None, :]
        s = jnp.where(valid, s, -jnp.inf)
        p = jax.nn.softmax(s, axis=-1).astype(v.dtype)
        return jnp.einsum('bthk,btkd->bthd', p, vsel)

    xs = (jnp.arange(nb), jnp.moveaxis(q, 1, 0), jnp.moveaxis(qi, 1, 0), jnp.moveaxis(wi, 1, 0))
    o = lax.map(block, xs)
    return jnp.moveaxis(o, 0, 1).reshape(b, l, ATT_WIDTH)


def cross_attention(hn, mn, w_xq, w_xkv, w_xo):
    b, l, _ = hn.shape
    m = mn.shape[1]
    q = (hn @ w_xq).reshape(b, l, X_HEADS, X_HEAD_DIM)
    kv = (mn @ w_xkv).reshape(b, m, 2, X_HEADS, X_HEAD_DIM)
    k, v = kv[:, :, 0], kv[:, :, 1]
    s = jnp.einsum('blhd,bmhd->bhlm', q, k).astype(jnp.float32) * (X_HEAD_DIM ** -0.5)
    p = jax.nn.softmax(s, axis=-1).astype(v.dtype)
    o = jnp.einsum('bhlm,bmhd->blhd', p, v).reshape(b, l, D_MODEL)
    return o @ w_xo


def hierarchical_moe(hn, w_rg, b_rg, w_re, b_re, w_e_gate, w_e_up, w_e_down):
    b, l, d = hn.shape
    t = hn.reshape(b * l, d)
    glog = (t @ w_rg).astype(jnp.float32) + b_rg.astype(jnp.float32)
    gprob = jax.nn.softmax(glog, axis=-1)
    g_sel = jnp.argmax(gprob, axis=-1)
    pg = jnp.max(gprob, axis=-1)
    elog = ((t @ w_re).astype(jnp.float32) + b_re.astype(jnp.float32)).reshape(-1, MOE_GROUPS, MOE_EXPERTS_PER_GROUP)
    elog_g = jnp.take_along_axis(elog, g_sel[:, None, None], axis=1)[:, 0, :]
    eprob = jax.nn.softmax(elog_g, axis=-1)
    topv, topi = lax.top_k(eprob, MOE_TOPK)
    wts = pg[:, None] * topv / jnp.sum(topv, axis=-1, keepdims=True)
    gid = g_sel[:, None] * MOE_EXPERTS_PER_GROUP + topi
    comb = jnp.sum(jax.nn.one_hot(gid, MOE_EXPERTS, dtype=jnp.float32) * wts[..., None], axis=1).astype(t.dtype)
    out = jnp.zeros_like(t)
    for e in range(MOE_EXPERTS):
        hid = jax.nn.silu(t @ w_e_gate[e]) * (t @ w_e_up[e])
        out = out + comb[:, e:e + 1] * (hid @ w_e_down[e])
    return out.reshape(b, l, d)


def hybrid_layer(x, mem, norm_mix, w_in, conv_w, conv_b, dt_bias, a_log, d_skip, ssm_norm,
                 w_ssm_out, w_att_out, w_mix_out, norm_x, norm_mem, w_xq, w_xkv, w_xo,
                 norm_moe, w_rg, b_rg, w_re, b_re, w_e_gate, w_e_up, w_e_down):
    u = rms_norm(x, norm_mix)
    proj = u @ w_in
    z, xbc, dt_raw, q, k, v, qi, ki, wi, g_ssm, g_att = jnp.split(proj, _split_points(), axis=-1)
    y_ssm = ssm_branch(z, xbc, dt_raw, conv_w, conv_b, dt_bias, a_log, d_skip, ssm_norm) @ w_ssm_out
    y_att = dsa_branch(q, k, v, qi, ki, wi) @ w_att_out
    mix = jax.nn.sigmoid(g_ssm) * y_ssm + jax.nn.sigmoid(g_att) * y_att
    x = x + mix @ w_mix_out
    x = x + cross_attention(rms_norm(x, norm_x), rms_norm(mem, norm_mem), w_xq, w_xkv, w_xo)
    x = x + hierarchical_moe(rms_norm(x, norm_moe), w_rg, b_rg, w_re, b_re, w_e_gate, w_e_up, w_e_down)
    return x


def setup_inputs(seed: int = 0) -> dict:
    key = jax.random.key(seed)
    ks = jax.random.split(key, 32)
    f32 = jnp.float32
    L = DEPTH

    def w(k, shape, fan_in):
        return jax.random.normal(k, shape, f32) * (fan_in ** -0.5)

    def gain(k, shape):
        return 1.0 + 0.02 * jax.random.normal(k, shape, f32)

    def small(k, shape, s=0.01):
        return s * jax.random.normal(k, shape, f32)

    u = jax.random.uniform(ks[7], (L, SSM_HEADS), f32)
    dt0 = jnp.exp(u * (np.log(0.1) - np.log(0.001)) + np.log(0.001)).astype(f32)
    dt_bias = dt0 + jnp.log(-jnp.expm1(-dt0))
    return {
        "x": jax.random.normal(ks[0], (BATCH, SEQ, D_MODEL), f32),
        "mem": jax.random.normal(ks[1], (BATCH, MEM_LEN, D_MODEL), f32),
        "norm_mix": gain(ks[2], (L, D_MODEL)),
        "w_in": w(ks[3], (L, D_MODEL, D_IN_PROJ), D_MODEL),
        "conv_w": w(ks[4], (L, SSM_CONV_DIM, SSM_CONV), SSM_CONV),
        "conv_b": small(ks[5], (L, SSM_CONV_DIM), 0.02),
        "dt_bias": dt_bias,
        "a_log": jnp.log(jax.random.uniform(ks[8], (L, SSM_HEADS), f32, 1.0, 16.0)),
        "d_skip": gain(ks[9], (L, SSM_HEADS)),
        "ssm_norm": gain(ks[10], (L, SSM_D_INNER)),
        "w_ssm_out": w(ks[11], (L, SSM_D_INNER, D_MODEL), SSM_D_INNER),
        "w_att_out": w(ks[12], (L, ATT_WIDTH, D_MODEL), ATT_WIDTH),
        "w_mix_out": w(ks[13], (L, D_MODEL, D_MODEL), D_MODEL),
        "norm_x": gain(ks[14], (L, D_MODEL)),
        "norm_mem": gain(ks[15], (L, D_MODEL)),
        "w_xq": w(ks[16], (L, D_MODEL, D_MODEL), D_MODEL),
        "w_xkv": w(ks[17], (L, D_MODEL, 2 * D_MODEL), D_MODEL),
        "w_xo": w(ks[18], (L, D_MODEL, D_MODEL), D_MODEL),
        "norm_moe": gain(ks[19], (L, D_MODEL)),
        "w_rg": w(ks[20], (L, D_MODEL, MOE_GROUPS), D_MODEL),
        "b_rg": small(ks[21], (L, MOE_GROUPS)),
        "w_re": w(ks[22], (L, D_MODEL, MOE_EXPERTS), D_MODEL),
        "b_re": small(ks[23], (L, MOE_EXPERTS)),
        "w_e_gate": w(ks[24], (L, MOE_EXPERTS, D_MODEL, MOE_HIDDEN), D_MODEL),
        "w_e_up": w(ks[25], (L, MOE_EXPERTS, D_MODEL, MOE_HIDDEN), D_MODEL),
        "w_e_down": w(ks[26], (L, MOE_EXPERTS, MOE_HIDDEN, D_MODEL), MOE_HIDDEN),
        "norm_final": gain(ks[27], (D_MODEL,)),
    }


def reference(x, mem, norm_mix, w_in, conv_w, conv_b, dt_bias, a_log, d_skip, ssm_norm,
              w_ssm_out, w_att_out, w_mix_out, norm_x, norm_mem, w_xq, w_xkv, w_xo,
              norm_moe, w_rg, b_rg, w_re, b_re, w_e_gate, w_e_up, w_e_down, norm_final):
    h = x
    for i in range(DEPTH):
        h = hybrid_layer(h, mem, norm_mix[i], w_in[i], conv_w[i], conv_b[i], dt_bias[i], a_log[i],
                         d_skip[i], ssm_norm[i], w_ssm_out[i], w_att_out[i], w_mix_out[i],
                         norm_x[i], norm_mem[i], w_xq[i], w_xkv[i], w_xo[i], norm_moe[i],
                         w_rg[i], b_rg[i], w_re[i], b_re[i], w_e_gate[i], w_e_up[i], w_e_down[i])
    return rms_norm(h, norm_final)
```

```python
import functools

import jax
import jax.numpy as jnp
from jax import lax
from jax.experimental import pallas as pl
from jax.experimental.pallas import tpu as pltpu

F32 = jnp.float32
BF16 = jnp.bfloat16
I32 = jnp.int32
EPS = 1e-6
HIGHEST = lax.Precision.HIGHEST

SSM_HEAD_DIM = 64
SSM_GROUPS = 4
SSM_STATE = 128
SSM_CHUNK = 128
ATT_HEAD_DIM = 64
IDX_HEAD_DIM = 64
TOPK_MAX = 256
Q_BLOCK = 128
NEG_INF = -1e30
X_HEADS = 4
MOE_GROUPS = 4
MOE_TOPK = 2

LANES = 128
VMEM_LIMIT = 56 * 1024 * 1024
EXPERT_TILE = 256
GATHER_ROWS = 256


def _cparams(*sem):
    return pltpu.CompilerParams(dimension_semantics=sem, vmem_limit_bytes=VMEM_LIMIT)


def _sigmoid(x):
    return 1.0 / (1.0 + jnp.exp(-x))


def _softplus(x):
    return jnp.maximum(x, 0.0) + jnp.log1p(jnp.exp(-jnp.abs(x)))


def _dot(a, b, precision=None):
    return jnp.dot(a, b, preferred_element_type=F32, precision=precision)


def _dot_nt(a, b):
    return lax.dot_general(a, b, (((1,), (1,)), ((), ())), preferred_element_type=F32)


def _rmsnorm_kernel(x_ref, g_ref, o_ref):
    x = x_ref[...]
    y = x * lax.rsqrt(jnp.mean(x * x, axis=-1, keepdims=True) + EPS) * g_ref[...]
    o_ref[...] = y.astype(o_ref.dtype)


def _rmsnorm(x, g, out_dtype, tm):
    m, d = x.shape
    return pl.pallas_call(
        _rmsnorm_kernel,
        grid=(m // tm,),
        in_specs=[pl.BlockSpec((tm, d), lambda i: (i, 0)),
                  pl.BlockSpec((1, d), lambda i: (0, 0))],
        out_specs=pl.BlockSpec((tm, d), lambda i: (i, 0)),
        out_shape=jax.ShapeDtypeStruct((m, d), out_dtype),
        compiler_params=_cparams("parallel"),
    )(x, g.reshape(1, d))


def _mm_kernel(a_ref, b_ref, o_ref):
    o_ref[...] = _dot(a_ref[...], b_ref[...]).astype(o_ref.dtype)


def _matmul(a, b, out_dtype, tm, tn):
    m, k = a.shape
    n = b.shape[1]
    return pl.pallas_call(
        _mm_kernel,
        grid=(n // tn, m // tm),
        in_specs=[pl.BlockSpec((tm, k), lambda j, i: (i, 0)),
                  pl.BlockSpec((k, tn), lambda j, i: (0, j))],
        out_specs=pl.BlockSpec((tm, tn), lambda j, i: (i, j)),
        out_shape=jax.ShapeDtypeStruct((m, n), out_dtype),
        compiler_params=_cparams("parallel", "parallel"),
    )(a, b)


def _ssd_kernel(z_ref, xbc_ref, dt_ref, dtt_ref, cw_ref, cb_ref, dtb_ref, dtbt_ref,
                alog_ref, alogt_ref, dskip_ref, norm_ref, e_ref, o_ref,
                tail_ref, st_ref, y_ref, *, heads, d_inner):
    q = SSM_CHUNK
    n = SSM_STATE
    hpg = heads // SSM_GROUPS
    gw = hpg * SSM_HEAD_DIM
    c = pl.program_id(1)

    @pl.when(c == 0)
    def _():
        tail_ref[...] = jnp.zeros_like(tail_ref)
        st_ref[...] = jnp.zeros_like(st_ref)

    x = xbc_ref[...]
    xcat = jnp.concatenate([tail_ref[...], x], axis=0)
    cw = cw_ref[...]
    kconv = cw.shape[0]
    acc = cb_ref[...] + cw[kconv - 1:kconv, :] * x
    for k in range(kconv - 1):
        off = 8 - (kconv - 1) + k
        acc = acc + cw[k:k + 1, :] * xcat[off:off + q, :]
    tail_ref[...] = x[q - 8:, :]
    xc = acc * _sigmoid(acc)
    xs = xc[:, :d_inner]
    bm = xc[:, d_inner:d_inner + SSM_GROUPS * n]
    cm = xc[:, d_inner + SSM_GROUPS * n:]

    dt = _softplus(dt_ref[...] + dtb_ref[...])
    da = dt * (-jnp.exp(alog_ref[...]))
    dtt = _softplus(dtt_ref[...] + dtbt_ref[...])
    dat = dtt * (-jnp.exp(alogt_ref[...]))
    rows = lax.broadcasted_iota(I32, (q, q), 0)
    cols = lax.broadcasted_iota(I32, (q, q), 1)
    causal = rows >= cols
    tril = jnp.where(causal, 1.0, 0.0)
    triu = jnp.where(rows <= cols, 1.0, 0.0)
    a_cs = _dot(tril, da, HIGHEST)
    a_cst = _dot(dat, triu, HIGHEST)
    expand = e_ref[...]
    dt_e = _dot(dt, expand, HIGHEST)
    acs_e = _dot(a_cs, expand, HIGHEST)
    expa = jnp.exp(acs_e)
    a_last = acs_e[q - 1:q, :]
    xdt = xs * dt_e
    xdt_b = xdt.astype(BF16)
    xdec_b = (xdt * jnp.exp(a_last - acs_e)).astype(BF16)
    st = st_ref[...]
    st_b = st.astype(BF16)
    lane = lax.broadcasted_iota(I32, (q, 2 * SSM_HEAD_DIM), 1)

    for g in range(SSM_GROUPS):
        bg = bm[:, g * n:(g + 1) * n]
        cg = cm[:, g * n:(g + 1) * n].astype(BF16)
        cb = _dot_nt(cg, bg.astype(BF16))
        gs = slice(g * gw, (g + 1) * gw)
        y_off = _dot(cg, st_b[:, gs])
        s_new = _dot(bg.T.astype(BF16), xdec_b[:, gs])
        st_ref[:, gs] = st[:, gs] * expa[q - 1:q, gs] + s_new
        for j in range(hpg // 2):
            h0 = g * hpg + 2 * j
            c0 = h0 * SSM_HEAD_DIM
            xp = xdt_b[:, c0:c0 + 2 * SSM_HEAD_DIM]
            parts = []
            for h in (h0, h0 + 1):
                seg = a_cs[:, h:h + 1] - a_cst[h:h + 1, :]
                lmat = jnp.exp(jnp.where(causal, seg, -jnp.inf))
                parts.append(_dot((cb * lmat).astype(BF16), xp))
            y_diag = jnp.where(lane < SSM_HEAD_DIM, parts[0], parts[1])
            cs = slice(c0, c0 + 2 * SSM_HEAD_DIM)
            y_ref[:, cs] = y_diag + y_off[:, c0 - g * gw:c0 - g * gw + 2 * SSM_HEAD_DIM] * expa[:, cs]

    y = y_ref[...] + dskip_ref[...] * xs
    zz = z_ref[...]
    yg = y * (zz * _sigmoid(zz))
    out = yg * lax.rsqrt(jnp.mean(yg * yg, axis=-1, keepdims=True) + EPS) * norm_ref[...]
    o_ref[...] = out.astype(o_ref.dtype)


def _ssd(z, xbc, small, dtt, conv_w, conv_b, dt_bias, a_log, d_skip, ssm_norm, batch, seq):
    heads = dt_bias.shape[0]
    d_inner = heads * SSM_HEAD_DIM
    conv_dim = xbc.shape[1]
    q = SSM_CHUNK
    nc = seq // q
    expand = jnp.repeat(jnp.eye(LANES, heads, dtype=F32), SSM_HEAD_DIM, axis=1)
    lane_pad = lambda v: jnp.pad(v.reshape(1, -1), ((0, 0), (0, LANES - heads)))
    row = lambda b, c: (b * nc + c, 0)
    const = lambda b, c: (0, 0)
    kern = functools.partial(_ssd_kernel, heads=heads, d_inner=d_inner)
    return pl.pallas_call(
        kern,
        grid=(batch, nc),
        in_specs=[
            pl.BlockSpec((q, d_inner), row),
            pl.BlockSpec((q, conv_dim), row),
            pl.BlockSpec((q, LANES), row),
            pl.BlockSpec((None, heads, q), lambda b, c: (b, 0, c)),
            pl.BlockSpec(conv_w.T.shape, const),
            pl.BlockSpec((1, conv_dim), const),
            pl.BlockSpec((1, LANES), const),
            pl.BlockSpec((heads, 1), const),
            pl.BlockSpec((1, LANES), const),
            pl.BlockSpec((heads, 1), const),
            pl.BlockSpec((1, d_inner), const),
            pl.BlockSpec((1, d_inner), const),
            pl.BlockSpec((LANES, d_inner), const),
        ],
        out_specs=pl.BlockSpec((q, d_inner), row),
        out_shape=jax.ShapeDtypeStruct((batch * seq, d_inner), BF16),
        scratch_shapes=[pltpu.VMEM((8, conv_dim), F32),
                        pltpu.VMEM((SSM_STATE, d_inner), F32),
                        pltpu.VMEM((q, d_inner), F32)],
        compiler_params=_cparams("parallel", "arbitrary"),
    )(z, xbc, small, dtt, conv_w.T, conv_b.reshape(1, -1),
      lane_pad(dt_bias), dt_bias.reshape(-1, 1), lane_pad(a_log), a_log.reshape(-1, 1),
      jnp.repeat(d_skip, SSM_HEAD_DIM).reshape(1, -1), ssm_norm.reshape(1, -1), expand)


def _dsa_kernel(q_ref, qi_ref, kv_ref, wi_ref, o_ref, key_ref, d_ref, *,
                topk, att_heads, idx_heads, wi_off):
    qb = Q_BLOCK
    seq = kv_ref.shape[0]
    blk = pl.program_id(1)
    hd = ATT_HEAD_DIM
    kk = kv_ref[:, 0:hd]
    vv = kv_ref[:, LANES:LANES + hd]
    ki = kv_ref[:, 2 * LANES:2 * LANES + IDX_HEAD_DIM]

    wi = wi_ref[:, wi_off:wi_off + idx_heads] * (idx_heads ** -0.5 * IDX_HEAD_DIM ** -0.5)
    isc = jnp.zeros((qb, seq), F32)
    for h in range(idx_heads):
        rel = _dot_nt(qi_ref[:, h * IDX_HEAD_DIM:(h + 1) * IDX_HEAD_DIM], ki)
        isc = isc + jnp.maximum(rel, 0.0) * wi[:, h:h + 1]
    tpos = blk * qb + lax.broadcasted_iota(I32, (qb, 1), 0)
    spos = lax.broadcasted_iota(I32, (1, seq), 1)
    causal = spos <= tpos
    masked = jnp.where(causal, isc, NEG_INF) + 0.0
    bits = pltpu.bitcast(masked, I32)
    key_ref[...] = bits ^ ((bits >> 31) & 0x7FFFFFFF)

    def count_ge(cand):
        return jnp.sum(jnp.where(key_ref[...] >= cand, 1.0, 0.0), axis=-1, keepdims=True)

    kf = float(topk)
    int_min = jnp.full((qb, 1), -2 ** 31, I32)
    thr0 = jnp.where(count_ge(jnp.zeros((qb, 1), I32)) >= kf, 0, int_min)

    def thr_body(j, thr):
        cand = thr + lax.shift_left(jnp.int32(1), 30 - j)
        return jnp.where(count_ge(cand) >= kf, cand, thr)

    thr = lax.fori_loop(0, 31, thr_body, thr0)

    key = key_ref[...]
    gt = key > thr
    eq = key == thr
    need = kf - jnp.sum(jnp.where(gt, 1.0, 0.0), axis=-1, keepdims=True)
    n_eq = jnp.sum(jnp.where(eq, 1.0, 0.0), axis=-1, keepdims=True)
    has_tie = jnp.max(jnp.where(n_eq > need, 1.0, 0.0)) > 0.0
    dist = (tpos - spos).astype(F32)

    @pl.when(jnp.logical_not(has_tie))
    def _():
        d_ref[...] = jnp.where(causal & (key >= thr), dist, jnp.inf)

    @pl.when(has_tie)
    def _():
        cut = jnp.zeros((qb, 1), I32)
        nbits = max(1, (seq - 1).bit_length())
        for b in range(nbits - 1, -1, -1):
            cand = cut + (1 << b)
            below = jnp.sum(jnp.where(eq & (spos < cand), 1.0, 0.0), axis=-1, keepdims=True)
            cut = jnp.where(below < need, cand, cut)
        sel = gt | (eq & (spos <= cut))
        d_ref[...] = jnp.where(causal & sel, dist, jnp.inf)

    qs = q_ref[...] * (hd ** -0.5)
    outs = []
    for h in range(att_heads):
        slope = 2.0 ** (-8.0 * (h + 1) / att_heads)
        s = _dot_nt(qs[:, h * hd:(h + 1) * hd], kk) - slope * d_ref[...]
        m = jnp.max(s, axis=-1, keepdims=True)
        p = jnp.exp(s - m)
        l = jnp.sum(p, axis=-1, keepdims=True)
        outs.append(_dot(p.astype(BF16), vv) / l)
    o_ref[...] = jnp.concatenate(outs, axis=-1).astype(o_ref.dtype)


def _dsa(qq, kvk, small, batch, seq, att_heads, idx_heads, wi_off):
    qb = Q_BLOCK
    nb = seq // qb
    att_w = att_heads * ATT_HEAD_DIM
    idx_w = idx_heads * IDX_HEAD_DIM
    assert att_w % idx_w == 0
    topk = min(TOPK_MAX, seq // 4)
    kern = functools.partial(_dsa_kernel, topk=topk, att_heads=att_heads,
                             idx_heads=idx_heads, wi_off=wi_off)
    return pl.pallas_call(
        kern,
        grid=(batch, nb),
        in_specs=[
            pl.BlockSpec((qb, att_w), lambda b, i: (b * nb + i, 0)),
            pl.BlockSpec((qb, idx_w), lambda b, i: (b * nb + i, att_w // idx_w)),
            pl.BlockSpec((seq, 3 * LANES), lambda b, i: (b, 0)),
            pl.BlockSpec((qb, LANES), lambda b, i: (b * nb + i, 0)),
        ],
        out_specs=pl.BlockSpec((qb, att_w), lambda b, i: (b * nb + i, 0)),
        out_shape=jax.ShapeDtypeStruct((batch * seq, att_w), BF16),
        scratch_shapes=[pltpu.VMEM((qb, seq), I32), pltpu.VMEM((qb, seq), F32)],
        compiler_params=_cparams("parallel", "parallel"),
    )(qq, qq, kvk, small)


def _merge_kernel(x_ref, ys_ref, oa_ref, g_ref, wso_ref, wao_ref, wmo_ref, nx_ref,
                  x1_ref, xn_ref):
    d = x_ref.shape[1]
    y_ssm = _dot(ys_ref[...], wso_ref[...])
    y_att = _dot(oa_ref[...], wao_ref[...])
    mix = _sigmoid(g_ref[:, :d]) * y_ssm + _sigmoid(g_ref[:, d:]) * y_att
    x1 = x_ref[...] + _dot(mix.astype(BF16), wmo_ref[...])
    x1_ref[...] = x1
    xn = x1 * lax.rsqrt(jnp.mean(x1 * x1, axis=-1, keepdims=True) + EPS) * nx_ref[...]
    xn_ref[...] = xn.astype(xn_ref.dtype)


def _merge(x, ys, oa, gates, wso, wao, wmo, norm_x, tm):
    t, d = x.shape
    row = lambda i: (i, 0)
    const = lambda i: (0, 0)
    return pl.pallas_call(
        _merge_kernel,
        grid=(t // tm,),
        in_specs=[pl.BlockSpec((tm, d), row),
                  pl.BlockSpec((tm, ys.shape[1]), row),
                  pl.BlockSpec((tm, oa.shape[1]), row),
                  pl.BlockSpec((tm, 2 * d), row),
                  pl.BlockSpec(wso.shape, const),
                  pl.BlockSpec(wao.shape, const),
                  pl.BlockSpec(wmo.shape, const),
                  pl.BlockSpec((1, d), const)],
        out_specs=[pl.BlockSpec((tm, d), row), pl.BlockSpec((tm, d), row)],
        out_shape=[jax.ShapeDtypeStruct((t, d), F32), jax.ShapeDtypeStruct((t, d), BF16)],
        compiler_params=_cparams("parallel"),
    )(x, ys, oa, gates, wso, wao, wmo, norm_x.reshape(1, d))


def _xattn_kernel(x1_ref, xn_ref, kv_ref, wq_ref, wo_ref, x2_ref):
    d = x1_ref.shape[1]
    hd = d // X_HEADS
    qf = _dot(xn_ref[...], wq_ref[...]) * (hd ** -0.5)
    qb = qf.astype(BF16)
    outs = []
    for h in range(X_HEADS):
        kh = kv_ref[:, h * hd:(h + 1) * hd]
        vh = kv_ref[:, d + h * hd:d + (h + 1) * hd]
        s = _dot_nt(qb[:, h * hd:(h + 1) * hd], kh)
        m = jnp.max(s, axis=-1, keepdims=True)
        p = jnp.exp(s - m)
        l = jnp.sum(p, axis=-1, keepdims=True)
        outs.append((_dot(p.astype(BF16), vh) / l).astype(BF16))
    o = jnp.concatenate(outs, axis=-1)
    x2_ref[...] = x1_ref[...] + _dot(o, wo_ref[...])


def _xattn(x1, xn, kvm, wq, wo, batch, seq, mem_len, tm):
    t, d = x1.shape
    nt = seq // tm
    row = lambda b, i: (b * nt + i, 0)
    const = lambda b, i: (0, 0)
    return pl.pallas_call(
        _xattn_kernel,
        grid=(batch, nt),
        in_specs=[pl.BlockSpec((tm, d), row),
                  pl.BlockSpec((tm, d), row),
                  pl.BlockSpec((mem_len, 2 * d), lambda b, i: (b, 0)),
                  pl.BlockSpec(wq.shape, const),
                  pl.BlockSpec(wo.shape, const)],
        out_specs=pl.BlockSpec((tm, d), row),
        out_shape=jax.ShapeDtypeStruct((t, d), F32),
        compiler_params=_cparams("parallel", "parallel"),
    )(x1, xn, kvm, wq, wo)


def _router_kernel(x_ref, g_ref, wr_ref, br_ref, xm_ref, route_ref, *, groups, per_group):
    x = x_ref[...]
    xm = x * lax.rsqrt(jnp.mean(x * x, axis=-1, keepdims=True) + EPS) * g_ref[...]
    xm_ref[...] = xm
    logits = _dot(xm, wr_ref[...], HIGHEST) + br_ref[...]
    lane = lax.broadcasted_iota(I32, logits.shape, 1)
    lanef = lane.astype(F32)
    far = float(LANES)
    ninf = -jnp.inf

    gl = jnp.where(lane < groups, logits, ninf)
    gmax = jnp.max(gl, axis=-1, keepdims=True)
    gsel = jnp.min(jnp.where(gl == gmax, lanef, far), axis=-1, keepdims=True)
    ge = jnp.exp(gl - gmax)
    pg = jnp.max(ge / jnp.sum(ge, axis=-1, keepdims=True), axis=-1, keepdims=True)

    lo = groups + per_group * gsel
    em = (lanef >= lo) & (lanef < lo + per_group)
    el = jnp.where(em, logits, ninf)
    ee = jnp.exp(el - jnp.max(el, axis=-1, keepdims=True))
    ep = jnp.where(em, ee / jnp.sum(ee, axis=-1, keepdims=True), -1.0)
    p1 = jnp.max(ep, axis=-1, keepdims=True)
    i1 = jnp.min(jnp.where(ep == p1, lanef, far), axis=-1, keepdims=True)
    ep2 = jnp.where(lanef == i1, -1.0, ep)
    p2 = jnp.max(ep2, axis=-1, keepdims=True)
    i2 = jnp.min(jnp.where(ep2 == p2, lanef, far), axis=-1, keepdims=True)
    w1 = pg * p1 / (p1 + p2)
    w2 = pg * p2 / (p1 + p2)
    route_ref[...] = jnp.where(lane == 0, i1 - groups,
                     jnp.where(lane == 1, i2 - groups,
                     jnp.where(lane == 2, w1,
                     jnp.where(lane == 3, w2, 0.0))))


def _router(x2, norm_moe, wr, br, groups, per_group, tm):
    t, d = x2.shape
    row = lambda i: (i, 0)
    const = lambda i: (0, 0)
    kern = functools.partial(_router_kernel, groups=groups, per_group=per_group)
    return pl.pallas_call(
        kern,
        grid=(t // tm,),
        in_specs=[pl.BlockSpec((tm, d), row), pl.BlockSpec((1, d), const),
                  pl.BlockSpec((d, LANES), const), pl.BlockSpec((1, LANES), const)],
        out_specs=[pl.BlockSpec((tm, d), row), pl.BlockSpec((tm, LANES), row)],
        out_shape=[jax.ShapeDtypeStruct((t, d), F32), jax.ShapeDtypeStruct((t, LANES), F32)],
        compiler_params=_cparams("parallel"),
    )(x2, norm_moe.reshape(1, d), wr, br)


def _row_copy(src_ref, src_row, dst_ref, dst_row, sem):
    return pltpu.make_async_copy(src_ref.at[pl.ds(src_row, 1), :],
                                 dst_ref.at[pl.ds(dst_row, 1), :], sem)


def _gather_kernel(idx_ref, src_ref, o_ref, sem):
    rows = o_ref.shape[0]

    def start(r, carry):
        _row_copy(src_ref, idx_ref[0, r], o_ref, r, sem).start()
        return carry

    def wait(r, carry):
        _row_copy(src_ref, 0, o_ref, r, sem).wait()
        return carry

    lax.fori_loop(0, rows, start, 0)
    lax.fori_loop(0, rows, wait, 0)


def _row_gather(src, idx):
    p = idx.shape[0]
    r = GATHER_ROWS
    w = src.shape[1]
    return pl.pallas_call(
        _gather_kernel,
        grid=(p // r,),
        in_specs=[pl.BlockSpec((None, 1, r), lambda i: (i, 0, 0), memory_space=pltpu.SMEM),
                  pl.BlockSpec(memory_space=pl.ANY)],
        out_specs=pl.BlockSpec((r, w), lambda i: (i, 0)),
        out_shape=jax.ShapeDtypeStruct((p, w), src.dtype),
        scratch_shapes=[pltpu.SemaphoreType.DMA(())],
        compiler_params=_cparams("arbitrary"),
    )(idx.reshape(p // r, 1, r), src)


def _scatter_kernel(idx_ref, src_ref, o_ref, sem):
    rows = src_ref.shape[0]

    def start(r, carry):
        dst = idx_ref[0, r]

        @pl.when(dst >= 0)
        def _():
            _row_copy(src_ref, r, o_ref, dst, sem).start()
        return carry

    def wait(r, carry):
        @pl.when(idx_ref[0, r] >= 0)
        def _():
            _row_copy(src_ref, r, o_ref, 0, sem).wait()
        return carry

    lax.fori_loop(0, rows, start, 0)
    lax.fori_loop(0, rows, wait, 0)


def _row_scatter(src, idx, n_out):
    p, w = src.shape
    r = GATHER_ROWS
    return pl.pallas_call(
        _scatter_kernel,
        grid=(p // r,),
        in_specs=[pl.BlockSpec((None, 1, r), lambda i: (i, 0, 0), memory_space=pltpu.SMEM),
                  pl.BlockSpec((r, w), lambda i: (i, 0))],
        out_specs=pl.BlockSpec(memory_space=pl.ANY),
        out_shape=jax.ShapeDtypeStruct((n_out, w), src.dtype),
        scratch_shapes=[pltpu.SemaphoreType.DMA(())],
        compiler_params=_cparams("arbitrary"),
    )(idx.reshape(p // r, 1, r), src)


def _expert_kernel(te_ref, tv_ref, x_ref, wg_ref, wu_ref, wd_ref, o_ref):
    i = pl.program_id(0)

    @pl.when(tv_ref[i] > 0)
    def _():
        xb = x_ref[...].astype(BF16)
        gate = _dot(xb, wg_ref[...])
        hid = gate * _sigmoid(gate) * _dot(xb, wu_ref[...])
        o_ref[...] = _dot(hid.astype(BF16), wd_ref[...])

    @pl.when(tv_ref[i] == 0)
    def _():
        o_ref[...] = jnp.zeros_like(o_ref)


def _experts(xs, tile_expert, tile_valid, wg, wu, wd):
    p, d = xs.shape
    tm = EXPERT_TILE
    hid = wg.shape[2]
    grid_spec = pltpu.PrefetchScalarGridSpec(
        num_scalar_prefetch=2,
        grid=(p // tm,),
        in_specs=[pl.BlockSpec((tm, d), lambda i, te, tv: (i, 0)),
                  pl.BlockSpec((None, d, hid), lambda i, te, tv: (te[i], 0, 0)),
                  pl.BlockSpec((None, d, hid), lambda i, te, tv: (te[i], 0, 0)),
                  pl.BlockSpec((None, hid, d), lambda i, te, tv: (te[i], 0, 0))],
        out_specs=pl.BlockSpec((tm, d), lambda i, te, tv: (i, 0)),
    )
    return pl.pallas_call(
        _expert_kernel,
        grid_spec=grid_spec,
        out_shape=jax.ShapeDtypeStruct((p, d), F32),
        compiler_params=_cparams("arbitrary"),
    )(tile_expert, tile_valid, xs, wg, wu, wd)


def _combine_kernel(x_ref, y_ref, route_ref, g_ref, o_ref):
    d = x_ref.shape[1]
    w1 = route_ref[:, 2:3]
    w2 = route_ref[:, 3:4]
    x = x_ref[...] + w1 * y_ref[:, :d] + w2 * y_ref[:, d:]
    o_ref[...] = x * lax.rsqrt(jnp.mean(x * x, axis=-1, keepdims=True) + EPS) * g_ref[...]


def _combine(x2, ypair, route, norm_final, tm):
    t, d = x2.shape
    row = lambda i: (i, 0)
    return pl.pallas_call(
        _combine_kernel,
        grid=(t // tm,),
        in_specs=[pl.BlockSpec((tm, d), row), pl.BlockSpec((tm, 2 * d), row),
                  pl.BlockSpec((tm, LANES), row), pl.BlockSpec((1, d), lambda i: (0, 0))],
        out_specs=pl.BlockSpec((tm, d), row),
        out_shape=jax.ShapeDtypeStruct((t, d), F32),
        compiler_params=_cparams("parallel"),
    )(x2, ypair, route, norm_final.reshape(1, d))


def _moe_plan(route, n_experts, tile):
    t = route.shape[0]
    pairs = MOE_TOPK * t
    n_tiles = pairs // tile + n_experts
    e_flat = route[:, :MOE_TOPK].astype(I32).reshape(pairs)
    order = jnp.argsort(e_flat, stable=True).astype(I32)
    counts = jnp.sum((e_flat[:, None] == jnp.arange(n_experts, dtype=I32)[None, :]).astype(I32), axis=0)
    tiles_per = (counts + tile - 1) // tile
    tile_end = jnp.cumsum(tiles_per)
    sorted_start = jnp.cumsum(counts) - counts
    tile_ids = jnp.arange(n_tiles, dtype=I32)
    tile_valid = (tile_ids < tile_end[-1]).astype(I32)
    tile_expert = jnp.minimum(jnp.searchsorted(tile_end, tile_ids, side='right').astype(I32), n_experts - 1)
    last_expert = tile_expert[jnp.maximum(tile_end[-1] - 1, 0)]
    tile_expert = jnp.where(tile_valid > 0, tile_expert, last_expert)
    tile_first = (tile_end - tiles_per)[tile_expert]
    row_in_group = ((tile_ids - tile_first) * tile)[:, None] + jnp.arange(tile, dtype=I32)[None, :]
    row_valid = (row_in_group < counts[tile_expert][:, None]) & (tile_valid[:, None] > 0)
    src = jnp.clip(sorted_start[tile_expert][:, None] + row_in_group, 0, pairs - 1)
    row_pair = jnp.where(row_valid, order[src], -1).reshape(-1)
    return tile_expert, tile_valid, row_pair


def kernel(x, mem, norm_mix, w_in, conv_w, conv_b, dt_bias, a_log, d_skip, ssm_norm, w_ssm_out,
           w_att_out, w_mix_out, norm_x, norm_mem, w_xq, w_xkv, w_xo, norm_moe, w_rg, b_rg, w_re,
           b_re, w_e_gate, w_e_up, w_e_down, norm_final):
    batch, seq, d = x.shape
    mem_len = mem.shape[1]
    depth = w_in.shape[0]
    t = batch * seq
    heads = dt_bias.shape[1]
    d_inner = heads * SSM_HEAD_DIM
    conv_dim = conv_w.shape[1]
    att_w = w_att_out.shape[1]
    att_heads = att_w // ATT_HEAD_DIM
    n_experts = w_re.shape[2]
    groups = w_rg.shape[2]
    per_group = n_experts // groups
    idx_heads = (w_in.shape[2] - (2 * d_inner + conv_dim - d_inner + heads + att_w
                                  + 2 * ATT_HEAD_DIM + IDX_HEAD_DIM + 2 * d)) // (IDX_HEAD_DIM + 1)
    idx_w = idx_heads * IDX_HEAD_DIM
    assert seq % SSM_CHUNK == 0 and seq % Q_BLOCK == 0 and heads + idx_heads <= LANES
    assert groups + n_experts <= LANES and t % 512 == 0 and heads % (2 * SSM_GROUPS) == 0

    h = x.reshape(t, d)
    memf = mem.reshape(batch * mem_len, d)
    for li in range(depth):
        sizes = (d_inner, conv_dim, heads, att_w, ATT_HEAD_DIM, ATT_HEAD_DIM, idx_w, IDX_HEAD_DIM,
                 idx_heads, d, d)
        offs = [0]
        for s in sizes:
            offs.append(offs[-1] + s)
        wi_ = w_in[li]
        col = lambda k: wi_[:, offs[k]:offs[k + 1]]
        pad = lambda a, n: jnp.pad(a, ((0, 0), (0, n - a.shape[1])))
        w_z = col(0).astype(BF16)
        w_xbc = col(1).astype(BF16)
        w_small = pad(jnp.concatenate([col(2), col(8)], axis=1), LANES).astype(BF16)
        w_qq = jnp.concatenate([col(3), col(6)], axis=1).astype(BF16)
        w_kvk = jnp.concatenate([pad(col(4), LANES), pad(col(5), LANES), pad(col(7), LANES)],
                                axis=1).astype(BF16)
        w_gates = jnp.concatenate([col(9), col(10)], axis=1).astype(BF16)

        u = _rmsnorm(h, norm_mix[li], BF16, 512)
        z = _matmul(u, w_z, F32, 512, 1024)
        xbc = _matmul(u, w_xbc, F32, 512, 1024)
        small = _matmul(u, w_small, F32, 512, LANES)
        qq = _matmul(u, w_qq, BF16, 512, att_w + idx_w)
        kvk = _matmul(u, w_kvk, BF16, 512, 3 * LANES)
        gates = _matmul(u, w_gates, F32, 512, 1024)

        dtt = jnp.swapaxes(small[:, :heads].reshape(batch, seq, heads), 1, 2)
        ys = _ssd(z, xbc, small, dtt, conv_w[li], conv_b[li], dt_bias[li], a_log[li], d_skip[li],
                  ssm_norm[li], batch, seq)
        oa = _dsa(qq, kvk, small, batch, seq, att_heads, idx_heads, heads)

        x1, xn1 = _merge(h, ys, oa, gates, w_ssm_out[li].astype(BF16), w_att_out[li].astype(BF16),
                         w_mix_out[li].astype(BF16), norm_x[li], 256)

        mn = _rmsnorm(memf, norm_mem[li], BF16, 512)
        kvm = _matmul(mn, w_xkv[li].astype(BF16), BF16, 512, 1024)
        x2 = _xattn(x1, xn1, kvm, w_xq[li].astype(BF16), w_xo[li].astype(BF16), batch, seq,
                    mem_len, 512)

        wr = pad(jnp.concatenate([w_rg[li], w_re[li]], axis=1), LANES)
        br = pad(jnp.concatenate([b_rg[li], b_re[li]]).reshape(1, -1), LANES)
        xm, route = _router(x2, norm_moe[li], wr, br, groups, per_group, 512)

        tile_expert, tile_valid, row_pair = _moe_plan(route, n_experts, EXPERT_TILE)
        xs = _row_gather(xm, jnp.maximum(row_pair, 0) // MOE_TOPK)
        ye = _experts(xs, tile_expert, tile_valid, w_e_gate[li].astype(BF16),
                      w_e_up[li].astype(BF16), w_e_down[li].astype(BF16))
        ypair = _row_scatter(ye, row_pair, MOE_TOPK * t).reshape(t, MOE_TOPK * d)
        last = li == depth - 1
        if last:
            h = _combine(x2, ypair, route, norm_final, 512)
        else:
            raise NotImplementedError("depth > 1 is not laid out here")
    return h.reshape(batch, seq, d)
```

```python
import functools
import math

import numpy as np
import jax
import jax.numpy as jnp
from jax import lax
from jax.experimental import pallas as pl
from jax.experimental.pallas import tpu as pltpu

F32 = jnp.float32
BF16 = jnp.bfloat16
I32 = jnp.int32
EPS = 1e-6
HIGHEST = lax.Precision.HIGHEST
LOG2E = math.log2(math.e)

SSM_HEAD_DIM = 64
SSM_GROUPS = 4
SSM_STATE = 128
SSM_CHUNK = 128
ATT_HEAD_DIM = 64
IDX_HEAD_DIM = 64
TOPK_MAX = 256
Q_BLOCK = 128
NEG_INF = -1e30
X_HEADS = 4
MOE_GROUPS = 4
MOE_TOPK = 2

LANES = 128
VMEM_LIMIT = 56 * 1024 * 1024
EXPERT_TILE = 256
DMA_UNROLL = 8
DSA_WIDTHS = 4
DSA_HEADS_PER_DOT = 4


def _cparams(*sem):
    return pltpu.CompilerParams(dimension_semantics=sem, vmem_limit_bytes=VMEM_LIMIT)


def _sigmoid(x):
    return 1.0 / (1.0 + jnp.exp(-x))


def _softplus(x):
    return jnp.maximum(x, 0.0) + jnp.log1p(jnp.exp(-jnp.abs(x)))


def _dot(a, b, precision=None):
    return jnp.dot(a, b, preferred_element_type=F32, precision=precision)


def _dot_nt(a, b):
    return lax.dot_general(a, b, (((1,), (1,)), ((), ())), preferred_element_type=F32)


def _reduce_rows(x, op, pair_op, chains=8):
    w, n = x.shape
    while w % (8 * chains):
        chains //= 2
    step = w // chains
    parts = [op(x[i * step:(i + 1) * step].reshape(step // 8, 8, n), axis=0) for i in range(chains)]
    while len(parts) > 1:
        parts = [pair_op(parts[i], parts[i + 1]) for i in range(0, len(parts), 2)]
    return op(parts[0], axis=0, keepdims=True)


def _rmsnorm_kernel(x_ref, g_ref, o_ref):
    x = x_ref[...]
    y = x * lax.rsqrt(jnp.mean(x * x, axis=-1, keepdims=True) + EPS) * g_ref[...]
    o_ref[...] = y.astype(o_ref.dtype)


def _rmsnorm(x, g, out_dtype, tm):
    m, d = x.shape
    return pl.pallas_call(
        _rmsnorm_kernel,
        grid=(m // tm,),
        in_specs=[pl.BlockSpec((tm, d), lambda i: (i, 0)),
                  pl.BlockSpec((1, d), lambda i: (0, 0))],
        out_specs=pl.BlockSpec((tm, d), lambda i: (i, 0)),
        out_shape=jax.ShapeDtypeStruct((m, d), out_dtype),
        compiler_params=_cparams("parallel"),
    )(x, g.reshape(1, d))


def _mm_kernel(a_ref, b_ref, o_ref):
    o_ref[...] = _dot(a_ref[...], b_ref[...]).astype(o_ref.dtype)


def _matmul(a, b, out_dtype, tm, tn):
    m, k = a.shape
    n = b.shape[1]
    return pl.pallas_call(
        _mm_kernel,
        grid=(n // tn, m // tm),
        in_specs=[pl.BlockSpec((tm, k), lambda j, i: (i, 0)),
                  pl.BlockSpec((k, tn), lambda j, i: (0, j))],
        out_specs=pl.BlockSpec((tm, tn), lambda j, i: (i, j)),
        out_shape=jax.ShapeDtypeStruct((m, n), out_dtype),
        compiler_params=_cparams("parallel", "parallel"),
    )(a, b)


def _ssd_kernel(z_ref, xbc_ref, dt_ref, dtt_ref, cw_ref, cb_ref, dtb_ref, dtbt_ref,
                alog_ref, alogt_ref, dskip_ref, norm_ref, e_ref, o_ref,
                tail_ref, st_ref, y_ref, *, heads, d_inner):
    q = SSM_CHUNK
    n = SSM_STATE
    hpg = heads // SSM_GROUPS
    gw = hpg * SSM_HEAD_DIM
    c = pl.program_id(1)

    @pl.when(c == 0)
    def _():
        tail_ref[...] = jnp.zeros_like(tail_ref)
        st_ref[...] = jnp.zeros_like(st_ref)

    x = xbc_ref[...]
    xcat = jnp.concatenate([tail_ref[...], x], axis=0)
    cw = cw_ref[...]
    kconv = cw.shape[0]
    acc = cb_ref[...] + cw[kconv - 1:kconv, :] * x
    for k in range(kconv - 1):
        off = 8 - (kconv - 1) + k
        acc = acc + cw[k:k + 1, :] * xcat[off:off + q, :]
    tail_ref[...] = x[q - 8:, :]
    xc = acc * _sigmoid(acc)
    xs = xc[:, :d_inner]
    bm = xc[:, d_inner:d_inner + SSM_GROUPS * n]
    cm = xc[:, d_inner + SSM_GROUPS * n:]

    dt = _softplus(dt_ref[...] + dtb_ref[...])
    da = dt * (-jnp.exp(alog_ref[...]))
    dtt = _softplus(dtt_ref[...] + dtbt_ref[...])
    dat = dtt * (-jnp.exp(alogt_ref[...]))
    rows = lax.broadcasted_iota(I32, (q, q), 0)
    cols = lax.broadcasted_iota(I32, (q, q), 1)
    causal = rows >= cols
    tril = jnp.where(causal, 1.0, 0.0)
    triu = jnp.where(rows <= cols, 1.0, 0.0)
    a_cs = _dot(tril, da, HIGHEST)
    a_cst = _dot(dat, triu, HIGHEST)
    expand = e_ref[...]
    dt_e = _dot(dt, expand, HIGHEST)
    acs_e = _dot(a_cs, expand, HIGHEST)
    expa = jnp.exp(acs_e)
    a_last = acs_e[q - 1:q, :]
    xdt = xs * dt_e
    xdt_b = xdt.astype(BF16)
    xdec_b = (xdt * jnp.exp(a_last - acs_e)).astype(BF16)
    st = st_ref[...]
    st_b = st.astype(BF16)
    lane = lax.broadcasted_iota(I32, (q, 2 * SSM_HEAD_DIM), 1)

    for g in range(SSM_GROUPS):
        bg = bm[:, g * n:(g + 1) * n]
        cg = cm[:, g * n:(g + 1) * n].astype(BF16)
        cb = _dot_nt(cg, bg.astype(BF16))
        gs = slice(g * gw, (g + 1) * gw)
        y_off = _dot(cg, st_b[:, gs])
        s_new = _dot(bg.T.astype(BF16), xdec_b[:, gs])
        st_ref[:, gs] = st[:, gs] * expa[q - 1:q, gs] + s_new
        for j in range(hpg // 2):
            h0 = g * hpg + 2 * j
            c0 = h0 * SSM_HEAD_DIM
            xp = xdt_b[:, c0:c0 + 2 * SSM_HEAD_DIM]
            parts = []
            for h in (h0, h0 + 1):
                seg = a_cs[:, h:h + 1] - a_cst[h:h + 1, :]
                lmat = jnp.exp(jnp.where(causal, seg, -jnp.inf))
                parts.append(_dot((cb * lmat).astype(BF16), xp))
            y_diag = jnp.where(lane < SSM_HEAD_DIM, parts[0], parts[1])
            cs = slice(c0, c0 + 2 * SSM_HEAD_DIM)
            y_ref[:, cs] = y_diag + y_off[:, c0 - g * gw:c0 - g * gw + 2 * SSM_HEAD_DIM] * expa[:, cs]

    y = y_ref[...] + dskip_ref[...] * xs
    zz = z_ref[...]
    yg = y * (zz * _sigmoid(zz))
    out = yg * lax.rsqrt(jnp.mean(yg * yg, axis=-1, keepdims=True) + EPS) * norm_ref[...]
    o_ref[...] = out.astype(o_ref.dtype)


def _ssd(z, xbc, small, dtt, conv_w, conv_b, dt_bias, a_log, d_skip, ssm_norm, batch, seq):
    heads = dt_bias.shape[0]
    d_inner = heads * SSM_HEAD_DIM
    conv_dim = xbc.shape[1]
    q = SSM_CHUNK
    nc = seq // q
    expand = jnp.repeat(jnp.eye(LANES, heads, dtype=F32), SSM_HEAD_DIM, axis=1)
    lane_pad = lambda v: jnp.pad(v.reshape(1, -1), ((0, 0), (0, LANES - heads)))
    row = lambda b, c: (b * nc + c, 0)
    const = lambda b, c: (0, 0)
    kern = functools.partial(_ssd_kernel, heads=heads, d_inner=d_inner)
    return pl.pallas_call(
        kern,
        grid=(batch, nc),
        in_specs=[
            pl.BlockSpec((q, d_inner), row),
            pl.BlockSpec((q, conv_dim), row),
            pl.BlockSpec((q, LANES), row),
            pl.BlockSpec((None, heads, q), lambda b, c: (b, 0, c)),
            pl.BlockSpec(conv_w.T.shape, const),
            pl.BlockSpec((1, conv_dim), const),
            pl.BlockSpec((1, LANES), const),
            pl.BlockSpec((heads, 1), const),
            pl.BlockSpec((1, LANES), const),
            pl.BlockSpec((heads, 1), const),
            pl.BlockSpec((1, d_inner), const),
            pl.BlockSpec((1, d_inner), const),
            pl.BlockSpec((LANES, d_inner), const),
        ],
        out_specs=pl.BlockSpec((q, d_inner), row),
        out_shape=jax.ShapeDtypeStruct((batch * seq, d_inner), BF16),
        scratch_shapes=[pltpu.VMEM((8, conv_dim), F32),
                        pltpu.VMEM((SSM_STATE, d_inner), F32),
                        pltpu.VMEM((q, d_inner), F32)],
        compiler_params=_cparams("parallel", "arbitrary"),
    )(z, xbc, small, dtt, conv_w.T, conv_b.reshape(1, -1),
      lane_pad(dt_bias), dt_bias.reshape(-1, 1), lane_pad(a_log), a_log.reshape(-1, 1),
      jnp.repeat(d_skip, SSM_HEAD_DIM).reshape(1, -1), ssm_norm.reshape(1, -1), expand)


def _order_key(x):
    bits = int(np.array(x, np.float32).view(np.int32))
    return bits ^ ((bits >> 31) & 0x7FFFFFFF)


def _dsa_consts(seq, att_heads):
    hd = ATT_HEAD_DIM
    assert seq <= 16 * 256 and 2 * hd == LANES
    qc = np.zeros((att_heads, LANES), np.float32)
    for h in range(att_heads):
        rest = np.float32(2.0 ** (-8.0 * (h + 1) / att_heads) * LOG2E)
        for i in range(3):
            piece = np.float32(rest.astype(jnp.bfloat16))
            rest = np.float32(rest - piece)
            qc[h, hd + i] = 16.0 * piece
            qc[h, hd + 3 + i] = piece
    pos = np.arange(seq)
    kpos = np.zeros((seq, LANES), np.float32)
    kpos[:, hd:hd + 3] = (pos // 16)[:, None]
    kpos[:, hd + 3:hd + 6] = (pos % 16)[:, None]
    return jnp.asarray(qc), jnp.asarray(kpos, dtype=BF16)


def _dsa_kernel(q_ref, qi_ref, ka_ref, kip_ref, vt_ref, wit_ref, qc_ref, o_ref,
                key_ref, mb_ref, qa_ref, *, topk, att_heads, idx_heads, widths):
    qb = Q_BLOCK
    seq = ka_ref.shape[0]
    nq = seq // qb
    blk = pl.program_id(1)
    hd = ATT_HEAD_DIM
    kf = float(topk)
    key_ni = _order_key(NEG_INF)
    left = lax.broadcasted_iota(I32, (qb, LANES), 1) < hd
    tpos = blk * qb + lax.broadcasted_iota(I32, (1, qb), 1)

    hpd = DSA_HEADS_PER_DOT
    for h in range(att_heads):
        qpair = q_ref[:, (h // 2) * LANES:(h // 2 + 1) * LANES].astype(F32)
        qh = qpair if h % 2 == 0 else pltpu.roll(qpair, hd, axis=1)
        qa_ref[h // hpd, (h % hpd) * qb:(h % hpd + 1) * qb, :] = jnp.where(
            left, qh, qc_ref[h:h + 1, :]).astype(BF16)

    def body(w):
        n_out = float(seq - w)
        spos = lax.broadcasted_iota(I32, (w, 1), 0)
        causal = spos <= tpos

        wit = wit_ref[...] * (idx_heads ** -0.5 * IDX_HEAD_DIM ** -0.5)
        isc = jnp.zeros((w, qb), F32)
        qi_rows = jnp.concatenate([qi_ref[:, g * LANES:(g + 1) * LANES]
                                   for g in range(idx_heads // 2)], axis=0)
        for par in range(2):
            rel = _dot_nt(kip_ref[:w, par * LANES:(par + 1) * LANES], qi_rows)
            for g in range(idx_heads // 2):
                h = 2 * g + par
                isc = isc + jnp.maximum(rel[:, g * qb:(g + 1) * qb], 0.0) * wit[h:h + 1, :]
        masked = jnp.where(causal, isc, NEG_INF) + 0.0
        bits = pltpu.bitcast(masked, I32)
        key_ref[:w, :] = bits ^ ((bits >> 31) & 0x7FFFFFFF)

        def count(mask):
            return _reduce_rows(jnp.where(mask, 1.0, 0.0), jnp.sum, jnp.add)

        def count_ge(cand):
            return count(key_ref[:w, :] >= cand) + jnp.where(key_ni >= cand, n_out, 0.0)

        int_min = jnp.full((1, qb), -2 ** 31, I32)
        thr0 = jnp.where(count_ge(jnp.zeros((1, qb), I32)) >= kf, 0, int_min)

        def thr_body(j, thr):
            cand = thr + lax.shift_left(jnp.int32(1), 30 - j)
            return jnp.where(count_ge(cand) >= kf, cand, thr)

        thr = lax.fori_loop(0, 31, thr_body, thr0)

        key = key_ref[:w, :]
        gt = key > thr
        eqc = (key == thr) & causal
        need = kf - count(gt) - jnp.where(key_ni > thr, n_out, 0.0)
        has_tie = jnp.max(jnp.where(count(eqc) > need, 1.0, 0.0)) > 0.0

        @pl.when(jnp.logical_not(has_tie))
        def _():
            mb_ref[:w, :] = jnp.where(causal & (key >= thr), 0.0, -jnp.inf)

        @pl.when(has_tie)
        def _():
            cut = jnp.zeros((1, qb), I32)
            for b in range(max(1, (w - 1).bit_length()) - 1, -1, -1):
                cand = cut + (1 << b)
                cut = jnp.where(count(eqc & (spos < cand)) < need, cand, cut)
            sel = gt | (eqc & (spos <= cut))
            mb_ref[:w, :] = jnp.where(causal & sel, 0.0, -jnp.inf)

        for jd in range(att_heads // hpd):
            s = _dot_nt(ka_ref[:w, :], qa_ref[jd])
            ps = []
            for e in range(hpd):
                sh = s[:, e * qb:(e + 1) * qb] + mb_ref[:w, :]
                ps.append(jnp.exp2(sh - _reduce_rows(sh, jnp.max, jnp.maximum)).astype(BF16))
            acc = _dot(vt_ref[:, :w], jnp.concatenate(ps, axis=1))
            o = acc * (1.0 / acc[hd:hd + 1, :])
            for e in range(0, hpd, 2):
                pair = jnp.where(left, o[:, e * qb:(e + 1) * qb].T,
                                 pltpu.roll(o[:, (e + 1) * qb:(e + 2) * qb].T, hd, axis=1))
                g = (jd * hpd + e) // 2
                o_ref[:, g * LANES:(g + 1) * LANES] = pair.astype(o_ref.dtype)

    per = nq // widths
    for v in range(widths):
        @pl.when((blk >= v * per) & (blk < (v + 1) * per))
        def _(v=v):
            body((v + 1) * per * qb)


def _dsa(qq, kvp, wit, batch, seq, att_heads, idx_heads):
    qb = Q_BLOCK
    nb = seq // qb
    hd = ATT_HEAD_DIM
    att_w = att_heads * hd
    idx_w = idx_heads * IDX_HEAD_DIM
    widths = min(DSA_WIDTHS, nb)
    assert att_w % idx_w == 0 and nb % widths == 0 and idx_heads % 2 == 0
    assert att_heads % DSA_HEADS_PER_DOT == 0 and DSA_HEADS_PER_DOT % 2 == 0
    assert 2 * IDX_HEAD_DIM == LANES and qb == LANES
    topk = min(TOPK_MAX, seq // 4)
    qc, kpos = _dsa_consts(seq, att_heads)
    ka = kvp[:, 2 * LANES:3 * LANES] + jnp.tile(kpos, (batch, 1))
    vt = jnp.swapaxes(kvp[:, 3 * LANES:].reshape(batch, seq, LANES), 1, 2)
    vt = vt.at[:, hd, :].set(1.0)
    kern = functools.partial(_dsa_kernel, topk=topk, att_heads=att_heads, idx_heads=idx_heads,
                             widths=widths)
    return pl.pallas_call(
        kern,
        grid=(batch, nb),
        in_specs=[
            pl.BlockSpec((qb, att_w), lambda b, i: (b * nb + i, 0)),
            pl.BlockSpec((qb, idx_w), lambda b, i: (b * nb + i, att_w // idx_w)),
            pl.BlockSpec((seq, LANES), lambda b, i: (b, 0)),
            pl.BlockSpec((seq, 2 * LANES), lambda b, i: (b, 0)),
            pl.BlockSpec((None, LANES, seq), lambda b, i: (b, 0, 0)),
            pl.BlockSpec((None, idx_heads, qb), lambda b, i: (b, 0, i)),
            pl.BlockSpec(qc.shape, lambda b, i: (0, 0)),
        ],
        out_specs=pl.BlockSpec((qb, att_w), lambda b, i: (b * nb + i, 0)),
        out_shape=jax.ShapeDtypeStruct((batch * seq, att_w), BF16),
        scratch_shapes=[pltpu.VMEM((seq, qb), I32),
                        pltpu.VMEM((seq, qb), F32),
                        pltpu.VMEM((att_heads // DSA_HEADS_PER_DOT, DSA_HEADS_PER_DOT * qb, LANES),
                                   BF16)],
        compiler_params=_cparams("parallel", "parallel"),
    )(qq, qq, ka, kvp, vt, wit, qc)


def _merge_kernel(x_ref, ys_ref, oa_ref, g_ref, wso_ref, wao_ref, wmo_ref, nx_ref,
                  x1_ref, xn_ref):
    d = x_ref.shape[1]
    y_ssm = _dot(ys_ref[...], wso_ref[...])
    y_att = _dot(oa_ref[...], wao_ref[...])
    mix = _sigmoid(g_ref[:, :d]) * y_ssm + _sigmoid(g_ref[:, d:]) * y_att
    x1 = x_ref[...] + _dot(mix.astype(BF16), wmo_ref[...])
    x1_ref[...] = x1
    xn = x1 * lax.rsqrt(jnp.mean(x1 * x1, axis=-1, keepdims=True) + EPS) * nx_ref[...]
    xn_ref[...] = xn.astype(xn_ref.dtype)


def _merge(x, ys, oa, gates, wso, wao, wmo, norm_x, tm):
    t, d = x.shape
    row = lambda i: (i, 0)
    const = lambda i: (0, 0)
    return pl.pallas_call(
        _merge_kernel,
        grid=(t // tm,),
        in_specs=[pl.BlockSpec((tm, d), row),
                  pl.BlockSpec((tm, ys.shape[1]), row),
                  pl.BlockSpec((tm, oa.shape[1]), row),
                  pl.BlockSpec((tm, 2 * d), row),
                  pl.BlockSpec(wso.shape, const),
                  pl.BlockSpec(wao.shape, const),
                  pl.BlockSpec(wmo.shape, const),
                  pl.BlockSpec((1, d), const)],
        out_specs=[pl.BlockSpec((tm, d), row), pl.BlockSpec((tm, d), row)],
        out_shape=[jax.ShapeDtypeStruct((t, d), F32), jax.ShapeDtypeStruct((t, d), BF16)],
        compiler_params=_cparams("parallel"),
    )(x, ys, oa, gates, wso, wao, wmo, norm_x.reshape(1, d))


def _xattn_kernel(x1_ref, xn_ref, kv_ref, wq_ref, wo_ref, x2_ref):
    d = x1_ref.shape[1]
    hd = d // X_HEADS
    qf = _dot(xn_ref[...], wq_ref[...]) * (hd ** -0.5)
    qb = qf.astype(BF16)
    outs = []
    for h in range(X_HEADS):
        kh = kv_ref[:, h * hd:(h + 1) * hd]
        vh = kv_ref[:, d + h * hd:d + (h + 1) * hd]
        s = _dot_nt(qb[:, h * hd:(h + 1) * hd], kh)
        m = jnp.max(s, axis=-1, keepdims=True)
        p = jnp.exp(s - m)
        l = jnp.sum(p, axis=-1, keepdims=True)
        outs.append((_dot(p.astype(BF16), vh) / l).astype(BF16))
    o = jnp.concatenate(outs, axis=-1)
    x2_ref[...] = x1_ref[...] + _dot(o, wo_ref[...])


def _xattn(x1, xn, kvm, wq, wo, batch, seq, mem_len, tm):
    t, d = x1.shape
    nt = seq // tm
    row = lambda b, i: (b * nt + i, 0)
    const = lambda b, i: (0, 0)
    return pl.pallas_call(
        _xattn_kernel,
        grid=(batch, nt),
        in_specs=[pl.BlockSpec((tm, d), row),
                  pl.BlockSpec((tm, d), row),
                  pl.BlockSpec((mem_len, 2 * d), lambda b, i: (b, 0)),
                  pl.BlockSpec(wq.shape, const),
                  pl.BlockSpec(wo.shape, const)],
        out_specs=pl.BlockSpec((tm, d), row),
        out_shape=jax.ShapeDtypeStruct((t, d), F32),
        compiler_params=_cparams("parallel", "parallel"),
    )(x1, xn, kvm, wq, wo)


def _router_kernel(x_ref, g_ref, wr_ref, br_ref, xm_ref, route_ref, *, groups, per_group):
    x = x_ref[...]
    xm = x * lax.rsqrt(jnp.mean(x * x, axis=-1, keepdims=True) + EPS) * g_ref[...]
    xm_ref[...] = xm
    logits = _dot(xm, wr_ref[...], HIGHEST) + br_ref[...]
    lane = lax.broadcasted_iota(I32, logits.shape, 1)
    lanef = lane.astype(F32)
    far = float(LANES)
    ninf = -jnp.inf

    gl = jnp.where(lane < groups, logits, ninf)
    gmax = jnp.max(gl, axis=-1, keepdims=True)
    gsel = jnp.min(jnp.where(gl == gmax, lanef, far), axis=-1, keepdims=True)
    ge = jnp.exp(gl - gmax)
    pg = jnp.max(ge / jnp.sum(ge, axis=-1, keepdims=True), axis=-1, keepdims=True)

    lo = groups + per_group * gsel
    em = (lanef >= lo) & (lanef < lo + per_group)
    el = jnp.where(em, logits, ninf)
    ee = jnp.exp(el - jnp.max(el, axis=-1, keepdims=True))
    ep = jnp.where(em, ee / jnp.sum(ee, axis=-1, keepdims=True), -1.0)
    p1 = jnp.max(ep, axis=-1, keepdims=True)
    i1 = jnp.min(jnp.where(ep == p1, lanef, far), axis=-1, keepdims=True)
    ep2 = jnp.where(lanef == i1, -1.0, ep)
    p2 = jnp.max(ep2, axis=-1, keepdims=True)
    i2 = jnp.min(jnp.where(ep2 == p2, lanef, far), axis=-1, keepdims=True)
    w1 = pg * p1 / (p1 + p2)
    w2 = pg * p2 / (p1 + p2)
    route_ref[...] = jnp.where(lane == 0, i1 - groups,
                     jnp.where(lane == 1, i2 - groups,
                     jnp.where(lane == 2, w1,
                     jnp.where(lane == 3, w2, 0.0))))


def _router(x2, norm_moe, wr, br, groups, per_group, tm):
    t, d = x2.shape
    row = lambda i: (i, 0)
    const = lambda i: (0, 0)
    kern = functools.partial(_router_kernel, groups=groups, per_group=per_group)
    return pl.pallas_call(
        kern,
        grid=(t // tm,),
        in_specs=[pl.BlockSpec((tm, d), row), pl.BlockSpec((1, d), const),
                  pl.BlockSpec((d, LANES), const), pl.BlockSpec((1, LANES), const)],
        out_specs=[pl.BlockSpec((tm, d), row), pl.BlockSpec((tm, LANES), row)],
        out_shape=[jax.ShapeDtypeStruct((t, d), F32), jax.ShapeDtypeStruct((t, LANES), F32)],
        compiler_params=_cparams("parallel"),
    )(x2, norm_moe.reshape(1, d), wr, br)


def _expert_kernel(te_ref, tv_ref, gcur_ref, gnext_ref, sidx_ref, xm_ref, wg_ref, wu_ref, wd_ref,
                   y_ref, xbuf, ybuf, gsem, ssem):
    i = pl.program_id(0)
    nt = pl.num_programs(0)
    slot = i % 2
    rows = xbuf.shape[1]
    nxt = jnp.minimum(i + 1, nt - 1)
    valid = tv_ref[i] > 0
    has_next = (i + 1 < nt) & (tv_ref[nxt] > 0)

    def gather_copy(src_row, s, r):
        return pltpu.make_async_copy(xm_ref.at[pl.ds(src_row, 1), :],
                                     xbuf.at[s, pl.ds(r, 1), :], gsem.at[s])

    def scatter_copy(s, r, dst_row):
        return pltpu.make_async_copy(ybuf.at[s, pl.ds(r, 1), :],
                                     y_ref.at[pl.ds(dst_row, 1), :], ssem.at[s])

    def per_row(fn):
        def it_body(it, carry):
            for u in range(DMA_UNROLL):
                fn(it * DMA_UNROLL + u, u % 2)
            return carry
        lax.fori_loop(0, rows // DMA_UNROLL, it_body, 0)

    def gather_start(idx_ref, s):
        per_row(lambda r, pri: gather_copy(idx_ref[0, r], s, r).start(priority=pri))

    def gather_wait(s):
        per_row(lambda r, pri: gather_copy(0, s, r).wait())

    def scatter_start(s):
        per_row(lambda r, pri: scatter_copy(s, r, sidx_ref[0, r]).start(priority=pri))

    def scatter_wait(s):
        per_row(lambda r, pri: scatter_copy(s, r, 0).wait())

    @pl.when(i == 0)
    def _():
        ybuf[1] = jnp.zeros(ybuf.shape[1:], ybuf.dtype)
        n_real = y_ref.shape[0] - 2 * rows
        fills = [pltpu.make_async_copy(ybuf.at[1], y_ref.at[pl.ds(n_real + k * rows, rows), :],
                                       ssem.at[1]) for k in range(2)]
        for cp in fills:
            cp.start()
        for cp in fills:
            cp.wait()

    @pl.when((i == 0) & valid)
    def _():
        gather_start(gcur_ref, 0)

    @pl.when(valid)
    def _():
        @pl.when(has_next)
        def _():
            gather_start(gnext_ref, 1 - slot)

        gather_wait(slot)

        @pl.when(i >= 2)
        def _():
            scatter_wait(slot)

        xb = xbuf[slot].astype(BF16)
        gate = _dot(xb, wg_ref[...])
        hid = gate * _sigmoid(gate) * _dot(xb, wu_ref[...])
        ybuf[slot] = _dot(hid.astype(BF16), wd_ref[...])
        scatter_start(slot)

        @pl.when(jnp.logical_not(has_next))
        def _():
            scatter_wait(slot)

            @pl.when(i >= 1)
            def _():
                scatter_wait(1 - slot)


def _experts(xm, tile_expert, tile_valid, gidx, sidx, wg, wu, wd, n_out):
    t, d = xm.shape
    tm = EXPERT_TILE
    nt = tile_expert.shape[0]
    hid = wg.shape[2]
    smem_rows = lambda imap: pl.BlockSpec((None, 1, tm), imap, memory_space=pltpu.SMEM)
    wspec = lambda shape: pl.BlockSpec((None,) + shape, lambda i, te, tv: (te[i], 0, 0))
    grid_spec = pltpu.PrefetchScalarGridSpec(
        num_scalar_prefetch=2,
        grid=(nt,),
        in_specs=[smem_rows(lambda i, te, tv: (i, 0, 0)),
                  smem_rows(lambda i, te, tv: (jnp.minimum(i + 1, nt - 1), 0, 0)),
                  smem_rows(lambda i, te, tv: (i, 0, 0)),
                  pl.BlockSpec(memory_space=pl.ANY),
                  wspec((d, hid)), wspec((d, hid)), wspec((hid, d))],
        out_specs=pl.BlockSpec(memory_space=pl.ANY),
        scratch_shapes=[pltpu.VMEM((2, tm, d), F32), pltpu.VMEM((2, tm, d), F32),
                        pltpu.SemaphoreType.DMA((2,)), pltpu.SemaphoreType.DMA((2,))],
    )
    g3 = gidx.reshape(nt, 1, tm)
    return pl.pallas_call(
        _expert_kernel,
        grid_spec=grid_spec,
        out_shape=jax.ShapeDtypeStruct((n_out, d), F32),
        compiler_params=_cparams("arbitrary"),
    )(tile_expert, tile_valid, g3, g3, sidx.reshape(nt, 1, tm), xm, wg, wu, wd)


def _combine_kernel(x_ref, y1_ref, y2_ref, route_ref, g_ref, o_ref):
    w1 = route_ref[:, 2:3]
    w2 = route_ref[:, 3:4]
    x = x_ref[...] + w1 * y1_ref[...] + w2 * y2_ref[...]
    o_ref[...] = x * lax.rsqrt(jnp.mean(x * x, axis=-1, keepdims=True) + EPS) * g_ref[...]


def _combine(x2, y, route, norm_final, tm):
    t, d = x2.shape
    row = lambda i: (i, 0)
    return pl.pallas_call(
        _combine_kernel,
        grid=(t // tm,),
        in_specs=[pl.BlockSpec((tm, d), row), pl.BlockSpec((tm, d), row),
                  pl.BlockSpec((tm, d), lambda i: (i + t // tm, 0)),
                  pl.BlockSpec((tm, LANES), row), pl.BlockSpec((1, d), lambda i: (0, 0))],
        out_specs=pl.BlockSpec((tm, d), row),
        out_shape=jax.ShapeDtypeStruct((t, d), F32),
        compiler_params=_cparams("parallel"),
    )(x2, y, y, route, norm_final.reshape(1, d))


def _moe_plan(route, n_experts, tile):
    t = route.shape[0]
    pairs = MOE_TOPK * t
    n_tiles = pairs // tile + n_experts
    e_flat = route[:, :MOE_TOPK].astype(I32).reshape(pairs)
    order = jnp.argsort(e_flat, stable=True).astype(I32)
    counts = jnp.sum((e_flat[:, None] == jnp.arange(n_experts, dtype=I32)[None, :]).astype(I32), axis=0)
    tiles_per = (counts + tile - 1) // tile
    tile_end = jnp.cumsum(tiles_per)
    sorted_start = jnp.cumsum(counts) - counts
    tile_ids = jnp.arange(n_tiles, dtype=I32)
    tile_valid = (tile_ids < tile_end[-1]).astype(I32)
    tile_expert = jnp.minimum(jnp.sum((tile_ids[:, None] >= tile_end[None, :]).astype(I32), axis=1),
                              n_experts - 1)
    last_expert = tile_expert[jnp.maximum(tile_end[-1] - 1, 0)]
    tile_expert = jnp.where(tile_valid > 0, tile_expert, last_expert)
    tile_first = (tile_end - tiles_per)[tile_expert]
    lane_row = jnp.arange(tile, dtype=I32)[None, :]
    row_in_group = ((tile_ids - tile_first) * tile)[:, None] + lane_row
    row_valid = (row_in_group < counts[tile_expert][:, None]) & (tile_valid[:, None] > 0)
    src = jnp.clip(sorted_start[tile_expert][:, None] + row_in_group, 0, pairs - 1)
    pair = order[src]
    token = jnp.where(row_valid, pair // MOE_TOPK, 0)
    spare = pairs + (tile_ids % 2)[:, None] * tile + lane_row
    dest = jnp.where(row_valid, (pair % MOE_TOPK) * t + pair // MOE_TOPK, spare)
    return tile_expert, tile_valid, token.reshape(-1), dest.reshape(-1)


def kernel(x, mem, norm_mix, w_in, conv_w, conv_b, dt_bias, a_log, d_skip, ssm_norm, w_ssm_out,
           w_att_out, w_mix_out, norm_x, norm_mem, w_xq, w_xkv, w_xo, norm_moe, w_rg, b_rg, w_re,
           b_re, w_e_gate, w_e_up, w_e_down, norm_final):
    batch, seq, d = x.shape
    mem_len = mem.shape[1]
    t = batch * seq
    heads = dt_bias.shape[1]
    d_inner = heads * SSM_HEAD_DIM
    conv_dim = conv_w.shape[1]
    att_w = w_att_out.shape[1]
    att_heads = att_w // ATT_HEAD_DIM
    n_experts = w_re.shape[2]
    groups = w_rg.shape[2]
    per_group = n_experts // groups
    idx_heads = (w_in.shape[2] - (d_inner + conv_dim + heads + att_w + 2 * ATT_HEAD_DIM
                                  + IDX_HEAD_DIM + 2 * d)) // (IDX_HEAD_DIM + 1)
    idx_w = idx_heads * IDX_HEAD_DIM
    assert w_in.shape[0] == 1, "one layer; the final norm is fused into the layer's last call"
    assert seq % SSM_CHUNK == 0 and seq % Q_BLOCK == 0 and heads + idx_heads <= LANES
    assert groups + n_experts <= LANES and t % 512 == 0 and heads % (2 * SSM_GROUPS) == 0
    li = 0

    sizes = (d_inner, conv_dim, heads, att_w, ATT_HEAD_DIM, ATT_HEAD_DIM, idx_w, IDX_HEAD_DIM,
             idx_heads, d, d)
    offs = [0]
    for s in sizes:
        offs.append(offs[-1] + s)
    wi_ = w_in[li]
    col = lambda k: wi_[:, offs[k]:offs[k + 1]]
    pad = lambda a, n: jnp.pad(a, ((0, 0), (0, n - a.shape[1])))
    lpad = lambda a, n: jnp.pad(a, ((0, 0), (n - a.shape[1], 0)))
    both = lambda a: [pad(a, LANES), lpad(a, LANES)]
    w_z = col(0).astype(BF16)
    w_xbc = col(1).astype(BF16)
    w_small = pad(jnp.concatenate([col(2), col(8)], axis=1), LANES).astype(BF16)
    w_qq = jnp.concatenate([col(3) * (LOG2E * ATT_HEAD_DIM ** -0.5), col(6)], axis=1).astype(BF16)
    w_kv = jnp.concatenate(both(col(7)) + [pad(col(4), LANES), pad(col(5), LANES)],
                           axis=1).astype(BF16)
    w_gates = jnp.concatenate([col(9), col(10)], axis=1).astype(BF16)

    h = x.reshape(t, d)
    u = _rmsnorm(h, norm_mix[li], BF16, 512)
    z = _matmul(u, w_z, F32, 512, 1024)
    xbc = _matmul(u, w_xbc, F32, 512, 1024)
    small = _matmul(u, w_small, F32, 512, LANES)
    qq = _matmul(u, w_qq, BF16, 512, att_w + idx_w)
    kvp = _matmul(u, w_kv, BF16, 512, 4 * LANES)
    gates = _matmul(u, w_gates, F32, 512, 1024)

    dtt = jnp.swapaxes(small[:, :heads].reshape(batch, seq, heads), 1, 2)
    ys = _ssd(z, xbc, small, dtt, conv_w[li], conv_b[li], dt_bias[li], a_log[li], d_skip[li],
              ssm_norm[li], batch, seq)
    wit = jnp.swapaxes(small[:, heads:heads + idx_heads].reshape(batch, seq, idx_heads), 1, 2)
    oa = _dsa(qq, kvp, wit, batch, seq, att_heads, idx_heads)

    x1, xn1 = _merge(h, ys, oa, gates, w_ssm_out[li].astype(BF16), w_att_out[li].astype(BF16),
                     w_mix_out[li].astype(BF16), norm_x[li], 256)

    mn = _rmsnorm(mem.reshape(batch * mem_len, d), norm_mem[li], BF16, 512)
    kvm = _matmul(mn, w_xkv[li].astype(BF16), BF16, 512, 1024)
    x2 = _xattn(x1, xn1, kvm, w_xq[li].astype(BF16), w_xo[li].astype(BF16), batch, seq,
                mem_len, 512)

    wr = pad(jnp.concatenate([w_rg[li], w_re[li]], axis=1), LANES)
    br = pad(jnp.concatenate([b_rg[li], b_re[li]]).reshape(1, -1), LANES)
    xm, route = _router(x2, norm_moe[li], wr, br, groups, per_group, 512)

    tile_expert, tile_valid, token, dest = _moe_plan(route, n_experts, EXPERT_TILE)
    ye = _experts(xm, tile_expert, tile_valid, token, dest, w_e_gate[li].astype(BF16),
                  w_e_up[li].astype(BF16), w_e_down[li].astype(BF16),
                  MOE_TOPK * t + 2 * EXPERT_TILE)
    out = _combine(x2, ye, route, norm_final, 512)
    return out.reshape(batch, seq, d)
```

```python
import functools
import math

import numpy as np
import jax
import jax.numpy as jnp
from jax import lax
from jax.experimental import pallas as pl
from jax.experimental.pallas import tpu as pltpu

F32 = jnp.float32
BF16 = jnp.bfloat16
I32 = jnp.int32
EPS = 1e-6
HIGHEST = lax.Precision.HIGHEST
LOG2E = math.log2(math.e)

SSM_HEAD_DIM = 64
SSM_GROUPS = 4
SSM_STATE = 128
SSM_CHUNK = 128
ATT_HEAD_DIM = 64
IDX_HEAD_DIM = 64
TOPK_MAX = 256
Q_BLOCK = 128
NEG_INF = -1e30
X_HEADS = 4
MOE_GROUPS = 4
MOE_TOPK = 2

LANES = 128
VMEM_LIMIT = 56 * 1024 * 1024
EXPERT_TILE = 256
DSA_WIDTHS = 4
DSA_HEADS_PER_DOT = 4


def _cparams(*sem):
    return pltpu.CompilerParams(dimension_semantics=sem, vmem_limit_bytes=VMEM_LIMIT)


def _sigmoid(x):
    return 1.0 / (1.0 + jnp.exp(-x))


def _softplus(x):
    return jnp.maximum(x, 0.0) + jnp.log1p(jnp.exp(-jnp.abs(x)))


def _dot(a, b, precision=None):
    return jnp.dot(a, b, preferred_element_type=F32, precision=precision)


def _split3(x):
    p0 = x.astype(BF16)
    r1 = x - p0.astype(F32)
    p1 = r1.astype(BF16)
    p2 = (r1 - p1.astype(F32)).astype(BF16)
    return p0, p1, p2


def _dot_exact_rhs(x, m):
    mb = m.astype(BF16)
    p0, p1, p2 = _split3(x)
    return _dot(p0, mb) + _dot(p1, mb) + _dot(p2, mb)


def _dot_exact_lhs(m, x):
    mb = m.astype(BF16)
    p0, p1, p2 = _split3(x)
    return _dot(mb, p0) + _dot(mb, p1) + _dot(mb, p2)


def _dot_nt(a, b):
    return lax.dot_general(a, b, (((1,), (1,)), ((), ())), preferred_element_type=F32)


def _reduce_rows(x, op, pair_op, chains=8):
    w, n = x.shape
    while w % (8 * chains):
        chains //= 2
    step = w // chains
    parts = [op(x[i * step:(i + 1) * step].reshape(step // 8, 8, n), axis=0) for i in range(chains)]
    while len(parts) > 1:
        parts = [pair_op(parts[i], parts[i + 1]) for i in range(0, len(parts), 2)]
    return op(parts[0], axis=0, keepdims=True)


def _rmsnorm_kernel(x_ref, g_ref, o_ref):
    x = x_ref[...]
    y = x * lax.rsqrt(jnp.mean(x * x, axis=-1, keepdims=True) + EPS) * g_ref[...]
    o_ref[...] = y.astype(o_ref.dtype)


def _rmsnorm(x, g, out_dtype, tm):
    m, d = x.shape
    return pl.pallas_call(
        _rmsnorm_kernel,
        grid=(m // tm,),
        in_specs=[pl.BlockSpec((tm, d), lambda i: (i, 0)),
                  pl.BlockSpec((1, d), lambda i: (0, 0))],
        out_specs=pl.BlockSpec((tm, d), lambda i: (i, 0)),
        out_shape=jax.ShapeDtypeStruct((m, d), out_dtype),
        compiler_params=_cparams("parallel"),
    )(x, g.reshape(1, d))


def _mm_kernel(a_ref, b_ref, o_ref):
    o_ref[...] = _dot(a_ref[...], b_ref[...]).astype(o_ref.dtype)


def _matmul(a, b, out_dtype, tm, tn):
    m, k = a.shape
    n = b.shape[1]
    return pl.pallas_call(
        _mm_kernel,
        grid=(n // tn, m // tm),
        in_specs=[pl.BlockSpec((tm, k), lambda j, i: (i, 0)),
                  pl.BlockSpec((k, tn), lambda j, i: (0, j))],
        out_specs=pl.BlockSpec((tm, tn), lambda j, i: (i, j)),
        out_shape=jax.ShapeDtypeStruct((m, n), out_dtype),
        compiler_params=_cparams("parallel", "parallel"),
    )(a, b)


def _ssd_kernel(z_ref, xbc_ref, dt_ref, dtt_ref, cw_ref, cb_ref, dtb_ref, dtbt_ref,
                alog_ref, alogt_ref, dskip_ref, norm_ref, e_ref, o_ref,
                tail_ref, st_ref, y_ref, *, heads, d_inner):
    q = SSM_CHUNK
    n = SSM_STATE
    hpg = heads // SSM_GROUPS
    gw = hpg * SSM_HEAD_DIM
    c = pl.program_id(1)

    @pl.when(c == 0)
    def _():
        tail_ref[...] = jnp.zeros_like(tail_ref)
        st_ref[...] = jnp.zeros_like(st_ref)

    x = xbc_ref[...]
    tl = tail_ref.shape[0]
    xcat = jnp.concatenate([tail_ref[...], x], axis=0)
    tail_ref[...] = x[q - tl:, :]
    cw = cw_ref[...]
    kconv = cw.shape[0]
    srow = lax.broadcasted_iota(I32, ((kconv - 1) * q, tl + q), 0)
    scol = lax.broadcasted_iota(I32, ((kconv - 1) * q, tl + q), 1)
    shift = jnp.where(scol - (tl - (kconv - 1)) == srow - (q - 1) * (srow // q), 1.0, 0.0)
    shifted = _dot(shift.astype(xcat.dtype), xcat)
    acc = cb_ref[...] + cw[kconv - 1:kconv, :] * x.astype(F32)
    for k in range(kconv - 1):
        acc = acc + cw[k:k + 1, :] * shifted[k * q:(k + 1) * q, :]
    xc = acc * _sigmoid(acc)
    xs = xc[:, :d_inner]
    bm = xc[:, d_inner:d_inner + SSM_GROUPS * n]
    cm = xc[:, d_inner + SSM_GROUPS * n:]

    dt = _softplus(dt_ref[...] + dtb_ref[...])
    da = dt * (-jnp.exp(alog_ref[...]))
    dtt = _softplus(dtt_ref[...] + dtbt_ref[...])
    dat = dtt * (-jnp.exp(alogt_ref[...]))
    rows = lax.broadcasted_iota(I32, (q, q), 0)
    cols = lax.broadcasted_iota(I32, (q, q), 1)
    causal = rows >= cols
    tril = jnp.where(causal, 1.0, 0.0)
    triu = jnp.where(rows <= cols, 1.0, 0.0)
    a_cs = _dot_exact_lhs(tril, da)
    a_cst = _dot_exact_rhs(dat, triu)
    expand = e_ref[...]
    dt_e = _dot_exact_rhs(dt, expand)
    acs_e = _dot_exact_rhs(a_cs, expand)
    expa = jnp.exp(acs_e)
    a_last = acs_e[q - 1:q, :]
    xdt = xs * dt_e
    xdt_b = xdt.astype(BF16)
    xdec_b = (xdt * jnp.exp(a_last - acs_e)).astype(BF16)
    st = st_ref[...]
    st_b = st.astype(BF16)
    lane = lax.broadcasted_iota(I32, (q, 2 * SSM_HEAD_DIM), 1)

    for g in range(SSM_GROUPS):
        bg = bm[:, g * n:(g + 1) * n]
        cg = cm[:, g * n:(g + 1) * n].astype(BF16)
        cb = _dot_nt(cg, bg.astype(BF16))
        gs = slice(g * gw, (g + 1) * gw)
        y_off = _dot(cg, st_b[:, gs])
        s_new = _dot(bg.T.astype(BF16), xdec_b[:, gs])
        st_ref[:, gs] = st[:, gs] * expa[q - 1:q, gs] + s_new
        for j in range(hpg // 2):
            h0 = g * hpg + 2 * j
            c0 = h0 * SSM_HEAD_DIM
            xp = xdt_b[:, c0:c0 + 2 * SSM_HEAD_DIM]
            parts = []
            for h in (h0, h0 + 1):
                seg = a_cs[:, h:h + 1] - a_cst[h:h + 1, :]
                lmat = jnp.exp(jnp.where(causal, seg, -jnp.inf))
                parts.append(_dot((cb * lmat).astype(BF16), xp))
            y_diag = jnp.where(lane < SSM_HEAD_DIM, parts[0], parts[1])
            cs = slice(c0, c0 + 2 * SSM_HEAD_DIM)
            y_ref[:, cs] = y_diag + y_off[:, c0 - g * gw:c0 - g * gw + 2 * SSM_HEAD_DIM] * expa[:, cs]

    y = y_ref[...] + dskip_ref[...] * xs
    zz = z_ref[...].astype(F32)
    yg = y * (zz * _sigmoid(zz))
    out = yg * lax.rsqrt(jnp.mean(yg * yg, axis=-1, keepdims=True) + EPS) * norm_ref[...]
    o_ref[...] = out.astype(o_ref.dtype)


def _ssd(z, xbc, small, dtt, conv_w, conv_b, dt_bias, a_log, d_skip, ssm_norm, batch, seq):
    heads = dt_bias.shape[0]
    d_inner = heads * SSM_HEAD_DIM
    conv_dim = xbc.shape[1]
    q = SSM_CHUNK
    nc = seq // q
    expand = jnp.repeat(jnp.eye(LANES, heads, dtype=BF16), SSM_HEAD_DIM, axis=1)
    lane_pad = lambda v: jnp.pad(v.reshape(1, -1), ((0, 0), (0, LANES - heads)))
    row = lambda b, c: (b * nc + c, 0)
    const = lambda b, c: (0, 0)
    kern = functools.partial(_ssd_kernel, heads=heads, d_inner=d_inner)
    return pl.pallas_call(
        kern,
        grid=(batch, nc),
        in_specs=[
            pl.BlockSpec((q, d_inner), row),
            pl.BlockSpec((q, conv_dim), row),
            pl.BlockSpec((q, LANES), row),
            pl.BlockSpec((None, heads, q), lambda b, c: (b, 0, c)),
            pl.BlockSpec(conv_w.T.shape, const),
            pl.BlockSpec((1, conv_dim), const),
            pl.BlockSpec((1, LANES), const),
            pl.BlockSpec((heads, 1), const),
            pl.BlockSpec((1, LANES), const),
            pl.BlockSpec((heads, 1), const),
            pl.BlockSpec((1, d_inner), const),
            pl.BlockSpec((1, d_inner), const),
            pl.BlockSpec((LANES, d_inner), const),
        ],
        out_specs=pl.BlockSpec((q, d_inner), row),
        out_shape=jax.ShapeDtypeStruct((batch * seq, d_inner), BF16),
        scratch_shapes=[pltpu.VMEM((16, conv_dim), xbc.dtype),
                        pltpu.VMEM((SSM_STATE, d_inner), F32),
                        pltpu.VMEM((q, d_inner), F32)],
        compiler_params=_cparams("parallel", "arbitrary"),
    )(z, xbc, small, dtt, conv_w.T, conv_b.reshape(1, -1),
      lane_pad(dt_bias), dt_bias.reshape(-1, 1), lane_pad(a_log), a_log.reshape(-1, 1),
      jnp.repeat(d_skip, SSM_HEAD_DIM).reshape(1, -1), ssm_norm.reshape(1, -1), expand)


def _order_key(x):
    bits = int(np.array(x, np.float32).view(np.int32))
    return bits ^ ((bits >> 31) & 0x7FFFFFFF)


def _dsa_consts(seq, att_heads):
    hd = ATT_HEAD_DIM
    assert seq <= 16 * 256 and 2 * hd == LANES
    qc = np.zeros((att_heads, LANES), np.float32)
    for h in range(att_heads):
        rest = np.float32(2.0 ** (-8.0 * (h + 1) / att_heads) * LOG2E)
        for i in range(3):
            piece = np.float32(rest.astype(jnp.bfloat16))
            rest = np.float32(rest - piece)
            qc[h, hd + i] = 16.0 * piece
            qc[h, hd + 3 + i] = piece
    pos = np.arange(seq)
    kpos = np.zeros((seq, LANES), np.float32)
    kpos[:, hd:hd + 3] = (pos // 16)[:, None]
    kpos[:, hd + 3:hd + 6] = (pos % 16)[:, None]
    return jnp.asarray(qc), jnp.asarray(kpos, dtype=BF16)


def _dsa_kernel(q_ref, qi_ref, ka_ref, kip_ref, vt_ref, wit_ref, qc_ref, o_ref,
                key_ref, mb_ref, qa_ref, *, topk, att_heads, idx_heads, widths):
    qb = Q_BLOCK
    seq = ka_ref.shape[0]
    nq = seq // qb
    blk = pl.program_id(1)
    hd = ATT_HEAD_DIM
    kf = float(topk)
    key_ni = _order_key(NEG_INF)
    left = lax.broadcasted_iota(I32, (qb, LANES), 1) < hd
    tpos = blk * qb + lax.broadcasted_iota(I32, (1, qb), 1)

    hpd = DSA_HEADS_PER_DOT
    for h in range(att_heads):
        qpair = q_ref[:, (h // 2) * LANES:(h // 2 + 1) * LANES].astype(F32)
        qh = qpair if h % 2 == 0 else pltpu.roll(qpair, hd, axis=1)
        qa_ref[h // hpd, (h % hpd) * qb:(h % hpd + 1) * qb, :] = jnp.where(
            left, qh, qc_ref[h:h + 1, :]).astype(BF16)

    def body(w):
        n_out = float(seq - w)
        spos = lax.broadcasted_iota(I32, (w, 1), 0)
        causal = spos <= tpos

        wit = wit_ref[...] * (idx_heads ** -0.5 * IDX_HEAD_DIM ** -0.5)
        isc = jnp.zeros((w, qb), F32)
        qi_rows = jnp.concatenate([qi_ref[:, g * LANES:(g + 1) * LANES]
                                   for g in range(idx_heads // 2)], axis=0)
        for par in range(2):
            rel = _dot_nt(kip_ref[:w, par * LANES:(par + 1) * LANES], qi_rows)
            for g in range(idx_heads // 2):
                h = 2 * g + par
                isc = isc + jnp.maximum(rel[:, g * qb:(g + 1) * qb], 0.0) * wit[h:h + 1, :]
        masked = jnp.where(causal, isc, NEG_INF) + 0.0
        bits = pltpu.bitcast(masked, I32)
        key_ref[:w, :] = bits ^ ((bits >> 31) & 0x7FFFFFFF)

        def count(mask):
            return _reduce_rows(jnp.where(mask, 1.0, 0.0), jnp.sum, jnp.add)

        def count_ge(cand):
            return count(key_ref[:w, :] >= cand) + jnp.where(key_ni >= cand, n_out, 0.0)

        int_min = jnp.full((1, qb), -2 ** 31, I32)
        thr0 = jnp.where(count_ge(jnp.zeros((1, qb), I32)) >= kf, 0, int_min)

        def thr_body(j, thr):
            cand = thr + lax.shift_left(jnp.int32(1), 30 - j)
            return jnp.where(count_ge(cand) >= kf, cand, thr)

        thr = lax.fori_loop(0, 31, thr_body, thr0)

        key = key_ref[:w, :]
        gt = key > thr
        eqc = (key == thr) & causal
        need = kf - count(gt) - jnp.where(key_ni > thr, n_out, 0.0)
        has_tie = jnp.max(jnp.where(count(eqc) > need, 1.0, 0.0)) > 0.0

        @pl.when(jnp.logical_not(has_tie))
        def _():
            mb_ref[:w, :] = jnp.where(causal & (key >= thr), 0.0, -jnp.inf)

        @pl.when(has_tie)
        def _():
            cut = jnp.zeros((1, qb), I32)
            for b in range(max(1, (w - 1).bit_length()) - 1, -1, -1):
                cand = cut + (1 << b)
                cut = jnp.where(count(eqc & (spos < cand)) < need, cand, cut)
            sel = gt | (eqc & (spos <= cut))
            mb_ref[:w, :] = jnp.where(causal & sel, 0.0, -jnp.inf)

        for jd in range(att_heads // hpd):
            s = _dot_nt(ka_ref[:w, :], qa_ref[jd])
            ps = []
            for e in range(hpd):
                sh = s[:, e * qb:(e + 1) * qb] + mb_ref[:w, :]
                ps.append(jnp.exp2((sh - _reduce_rows(sh, jnp.max, jnp.maximum)).astype(BF16)))
            acc = _dot(vt_ref[:, :w], jnp.concatenate(ps, axis=1))
            o = acc * (1.0 / acc[hd:hd + 1, :])
            for e in range(0, hpd, 2):
                pair = jnp.where(left, o[:, e * qb:(e + 1) * qb].T,
                                 pltpu.roll(o[:, (e + 1) * qb:(e + 2) * qb].T, hd, axis=1))
                g = (jd * hpd + e) // 2
                o_ref[:, g * LANES:(g + 1) * LANES] = pair.astype(o_ref.dtype)

    per = nq // widths
    for v in range(widths):
        @pl.when((blk >= v * per) & (blk < (v + 1) * per))
        def _(v=v):
            body((v + 1) * per * qb)


def _dsa(qq, kvp, wit, batch, seq, att_heads, idx_heads):
    qb = Q_BLOCK
    nb = seq // qb
    hd = ATT_HEAD_DIM
    att_w = att_heads * hd
    idx_w = idx_heads * IDX_HEAD_DIM
    widths = min(DSA_WIDTHS, nb)
    assert att_w % idx_w == 0 and nb % widths == 0 and idx_heads % 2 == 0
    assert att_heads % DSA_HEADS_PER_DOT == 0 and DSA_HEADS_PER_DOT % 2 == 0
    assert 2 * IDX_HEAD_DIM == LANES and qb == LANES
    topk = min(TOPK_MAX, seq // 4)
    qc, kpos = _dsa_consts(seq, att_heads)
    ka = kvp[:, 2 * LANES:3 * LANES] + jnp.tile(kpos, (batch, 1))
    vt = jnp.swapaxes(kvp[:, 3 * LANES:].reshape(batch, seq, LANES), 1, 2)
    vt = vt.at[:, hd, :].set(1.0)
    kern = functools.partial(_dsa_kernel, topk=topk, att_heads=att_heads, idx_heads=idx_heads,
                             widths=widths)
    return pl.pallas_call(
        kern,
        grid=(batch, nb),
        in_specs=[
            pl.BlockSpec((qb, att_w), lambda b, i: (b * nb + i, 0)),
            pl.BlockSpec((qb, idx_w), lambda b, i: (b * nb + i, att_w // idx_w)),
            pl.BlockSpec((seq, LANES), lambda b, i: (b, 0)),
            pl.BlockSpec((seq, 2 * LANES), lambda b, i: (b, 0)),
            pl.BlockSpec((None, LANES, seq), lambda b, i: (b, 0, 0)),
            pl.BlockSpec((None, idx_heads, qb), lambda b, i: (b, 0, i)),
            pl.BlockSpec(qc.shape, lambda b, i: (0, 0)),
        ],
        out_specs=pl.BlockSpec((qb, att_w), lambda b, i: (b * nb + i, 0)),
        out_shape=jax.ShapeDtypeStruct((batch * seq, att_w), BF16),
        scratch_shapes=[pltpu.VMEM((seq, qb), I32),
                        pltpu.VMEM((seq, qb), F32),
                        pltpu.VMEM((att_heads // DSA_HEADS_PER_DOT, DSA_HEADS_PER_DOT * qb, LANES),
                                   BF16)],
        compiler_params=_cparams("parallel", "parallel"),
    )(qq, qq, ka, kvp, vt, wit, qc)


def _merge_kernel(x_ref, ys_ref, oa_ref, g_ref, wso_ref, wao_ref, wmo_ref, nx_ref,
                  x1_ref, xn_ref):
    d = x_ref.shape[1]
    y_ssm = _dot(ys_ref[...], wso_ref[...])
    y_att = _dot(oa_ref[...], wao_ref[...])
    g = g_ref[...].astype(F32)
    mix = _sigmoid(g[:, :d]) * y_ssm + _sigmoid(g[:, d:]) * y_att
    x1 = x_ref[...] + _dot(mix.astype(BF16), wmo_ref[...])
    x1_ref[...] = x1
    xn = x1 * lax.rsqrt(jnp.mean(x1 * x1, axis=-1, keepdims=True) + EPS) * nx_ref[...]
    xn_ref[...] = xn.astype(xn_ref.dtype)


def _merge(x, ys, oa, gates, wso, wao, wmo, norm_x, tm):
    t, d = x.shape
    row = lambda i: (i, 0)
    const = lambda i: (0, 0)
    return pl.pallas_call(
        _merge_kernel,
        grid=(t // tm,),
        in_specs=[pl.BlockSpec((tm, d), row),
                  pl.BlockSpec((tm, ys.shape[1]), row),
                  pl.BlockSpec((tm, oa.shape[1]), row),
                  pl.BlockSpec((tm, 2 * d), row),
                  pl.BlockSpec(wso.shape, const),
                  pl.BlockSpec(wao.shape, const),
                  pl.BlockSpec(wmo.shape, const),
                  pl.BlockSpec((1, d), const)],
        out_specs=[pl.BlockSpec((tm, d), row), pl.BlockSpec((tm, d), row)],
        out_shape=[jax.ShapeDtypeStruct((t, d), F32), jax.ShapeDtypeStruct((t, d), BF16)],
        compiler_params=_cparams("parallel"),
    )(x, ys, oa, gates, wso, wao, wmo, norm_x.reshape(1, d))


def _xattn_kernel(x1_ref, xn_ref, kv_ref, wq_ref, wo_ref, x2_ref):
    d = x1_ref.shape[1]
    hd = d // X_HEADS
    qf = _dot(xn_ref[...], wq_ref[...]) * (hd ** -0.5)
    qb = qf.astype(BF16)
    outs = []
    for h in range(X_HEADS):
        kh = kv_ref[:, h * hd:(h + 1) * hd]
        vh = kv_ref[:, d + h * hd:d + (h + 1) * hd]
        s = _dot_nt(qb[:, h * hd:(h + 1) * hd], kh)
        m = jnp.max(s, axis=-1, keepdims=True)
        p = jnp.exp(s - m)
        l = jnp.sum(p, axis=-1, keepdims=True)
        outs.append((_dot(p.astype(BF16), vh) / l).astype(BF16))
    o = jnp.concatenate(outs, axis=-1)
    x2_ref[...] = x1_ref[...] + _dot(o, wo_ref[...])


def _xattn(x1, xn, kvm, wq, wo, batch, seq, mem_len, tm):
    t, d = x1.shape
    nt = seq // tm
    row = lambda b, i: (b * nt + i, 0)
    const = lambda b, i: (0, 0)
    return pl.pallas_call(
        _xattn_kernel,
        grid=(batch, nt),
        in_specs=[pl.BlockSpec((tm, d), row),
                  pl.BlockSpec((tm, d), row),
                  pl.BlockSpec((mem_len, 2 * d), lambda b, i: (b, 0)),
                  pl.BlockSpec(wq.shape, const),
                  pl.BlockSpec(wo.shape, const)],
        out_specs=pl.BlockSpec((tm, d), row),
        out_shape=jax.ShapeDtypeStruct((t, d), F32),
        compiler_params=_cparams("parallel", "parallel"),
    )(x1, xn, kvm, wq, wo)


def _router_kernel(x_ref, g_ref, wr_ref, br_ref, xm_ref, route_ref, *, groups, per_group):
    x = x_ref[...]
    xm = x * lax.rsqrt(jnp.mean(x * x, axis=-1, keepdims=True) + EPS) * g_ref[...]
    xm_ref[...] = xm
    logits = _dot(xm, wr_ref[...], HIGHEST) + br_ref[...]
    lane = lax.broadcasted_iota(I32, logits.shape, 1)
    lanef = lane.astype(F32)
    far = float(LANES)
    ninf = -jnp.inf

    gl = jnp.where(lane < groups, logits, ninf)
    gmax = jnp.max(gl, axis=-1, keepdims=True)
    gsel = jnp.min(jnp.where(gl == gmax, lanef, far), axis=-1, keepdims=True)
    ge = jnp.exp(gl - gmax)
    pg = jnp.max(ge / jnp.sum(ge, axis=-1, keepdims=True), axis=-1, keepdims=True)

    lo = groups + per_group * gsel
    em = (lanef >= lo) & (lanef < lo + per_group)
    el = jnp.where(em, logits, ninf)
    ee = jnp.exp(el - jnp.max(el, axis=-1, keepdims=True))
    ep = jnp.where(em, ee / jnp.sum(ee, axis=-1, keepdims=True), -1.0)
    p1 = jnp.max(ep, axis=-1, keepdims=True)
    i1 = jnp.min(jnp.where(ep == p1, lanef, far), axis=-1, keepdims=True)
    ep2 = jnp.where(lanef == i1, -1.0, ep)
    p2 = jnp.max(ep2, axis=-1, keepdims=True)
    i2 = jnp.min(jnp.where(ep2 == p2, lanef, far), axis=-1, keepdims=True)
    w1 = pg * p1 / (p1 + p2)
    w2 = pg * p2 / (p1 + p2)
    route_ref[...] = jnp.where(lane == 0, i1 - groups,
                     jnp.where(lane == 1, i2 - groups,
                     jnp.where(lane == 2, w1,
                     jnp.where(lane == 3, w2, 0.0))))


def _router(x2, norm_moe, wr, br, groups, per_group, tm):
    t, d = x2.shape
    row = lambda i: (i, 0)
    const = lambda i: (0, 0)
    kern = functools.partial(_router_kernel, groups=groups, per_group=per_group)
    return pl.pallas_call(
        kern,
        grid=(t // tm,),
        in_specs=[pl.BlockSpec((tm, d), row), pl.BlockSpec((1, d), const),
                  pl.BlockSpec((d, LANES), const), pl.BlockSpec((1, LANES), const)],
        out_specs=[pl.BlockSpec((tm, d), row), pl.BlockSpec((tm, LANES), row)],
        out_shape=[jax.ShapeDtypeStruct((t, d), F32), jax.ShapeDtypeStruct((t, LANES), F32)],
        compiler_params=_cparams("parallel"),
    )(x2, norm_moe.reshape(1, d), wr, br)


def _expert_kernel(te_ref, tv_ref, gcur_ref, gnext_ref, sidx_ref, xm_ref, wg_ref, wu_ref, wd_ref,
                   y_ref, xbuf, ybuf, gsem, ssem):
    i = pl.program_id(0)
    nt = pl.num_programs(0)
    slot = i % 2
    groups8 = xbuf.shape[1]
    rows = 8 * groups8
    d = xbuf.shape[3]
    nxt = jnp.minimum(i + 1, nt - 1)
    valid = tv_ref[i] > 0
    has_next = (i + 1 < nt) & (tv_ref[nxt] > 0)

    def gather_copy(src_row, s, g8, u):
        return pltpu.make_async_copy(xm_ref.at[pl.ds(src_row, 1), :],
                                     xbuf.at[s, g8, pl.ds(u, 1), :], gsem.at[s])

    def scatter_copy(s, g8, u, dst_row):
        return pltpu.make_async_copy(ybuf.at[s, g8, pl.ds(u, 1), :],
                                     y_ref.at[pl.ds(dst_row, 1), :], ssem.at[s])

    def per_row(fn):
        def g8_body(g8, carry):
            for u in range(8):
                fn(g8, u)
            return carry
        lax.fori_loop(0, groups8, g8_body, 0)

    def gather_start(idx_ref, s):
        per_row(lambda g8, u: gather_copy(idx_ref[0, g8 * 8 + u], s, g8, u).start(priority=u % 2))

    def gather_wait(s):
        per_row(lambda g8, u: gather_copy(0, s, g8, u).wait())

    def scatter_start(s):
        per_row(lambda g8, u: scatter_copy(s, g8, u, sidx_ref[0, g8 * 8 + u]).start(priority=u % 2))

    def scatter_wait(s):
        per_row(lambda g8, u: scatter_copy(s, g8, u, 0).wait())

    @pl.when(i == 0)
    def _():
        ybuf[1] = jnp.zeros(ybuf.shape[1:], ybuf.dtype)
        n_real = y_ref.shape[0] - 2 * rows

        def fill_copy(g):
            return pltpu.make_async_copy(ybuf.at[1, g % groups8],
                                         y_ref.at[pl.ds(n_real + g * 8, 8), :], ssem.at[1])

        def fill_start(g, carry):
            fill_copy(g).start()
            return carry

        def fill_wait(g, carry):
            fill_copy(g).wait()
            return carry

        lax.fori_loop(0, 2 * groups8, fill_start, 0)
        lax.fori_loop(0, 2 * groups8, fill_wait, 0)

    @pl.when((i == 0) & valid)
    def _():
        gather_start(gcur_ref, 0)

    @pl.when(valid)
    def _():
        @pl.when(has_next)
        def _():
            gather_start(gnext_ref, 1 - slot)

        gather_wait(slot)

        @pl.when(i >= 2)
        def _():
            scatter_wait(slot)

        xb = xbuf[slot].reshape(rows, d).astype(BF16)
        gate = _dot(xb, wg_ref[...])
        hid = gate * _sigmoid(gate) * _dot(xb, wu_ref[...])
        ybuf[slot] = _dot(hid.astype(BF16), wd_ref[...]).reshape(groups8, 8, d)
        scatter_start(slot)

        @pl.when(jnp.logical_not(has_next))
        def _():
            scatter_wait(slot)

            @pl.when(i >= 1)
            def _():
                scatter_wait(1 - slot)


def _experts(xm, tile_expert, tile_valid, gidx, sidx, wg, wu, wd, n_out):
    t, d = xm.shape
    tm = EXPERT_TILE
    nt = tile_expert.shape[0]
    hid = wg.shape[2]
    smem_rows = lambda imap: pl.BlockSpec((None, 1, tm), imap, memory_space=pltpu.SMEM)
    wspec = lambda shape: pl.BlockSpec((None,) + shape, lambda i, te, tv: (te[i], 0, 0))
    grid_spec = pltpu.PrefetchScalarGridSpec(
        num_scalar_prefetch=2,
        grid=(nt,),
        in_specs=[smem_rows(lambda i, te, tv: (i, 0, 0)),
                  smem_rows(lambda i, te, tv: (jnp.minimum(i + 1, nt - 1), 0, 0)),
                  smem_rows(lambda i, te, tv: (i, 0, 0)),
                  pl.BlockSpec(memory_space=pl.ANY),
                  wspec((d, hid)), wspec((d, hid)), wspec((hid, d))],
        out_specs=pl.BlockSpec(memory_space=pl.ANY),
        scratch_shapes=[pltpu.VMEM((2, tm // 8, 8, d), F32), pltpu.VMEM((2, tm // 8, 8, d), F32),
                        pltpu.SemaphoreType.DMA((2,)), pltpu.SemaphoreType.DMA((2,))],
    )
    g3 = gidx.reshape(nt, 1, tm)
    return pl.pallas_call(
        _expert_kernel,
        grid_spec=grid_spec,
        out_shape=jax.ShapeDtypeStruct((n_out, d), F32),
        compiler_params=_cparams("arbitrary"),
    )(tile_expert, tile_valid, g3, g3, sidx.reshape(nt, 1, tm), xm, wg, wu, wd)


def _combine_kernel(x_ref, y1_ref, y2_ref, route_ref, g_ref, o_ref):
    w1 = route_ref[:, 2:3]
    w2 = route_ref[:, 3:4]
    x = x_ref[...] + w1 * y1_ref[...] + w2 * y2_ref[...]
    o_ref[...] = x * lax.rsqrt(jnp.mean(x * x, axis=-1, keepdims=True) + EPS) * g_ref[...]


def _combine(x2, y, route, norm_final, tm):
    t, d = x2.shape
    row = lambda i: (i, 0)
    return pl.pallas_call(
        _combine_kernel,
        grid=(t // tm,),
        in_specs=[pl.BlockSpec((tm, d), row), pl.BlockSpec((tm, d), row),
                  pl.BlockSpec((tm, d), lambda i: (i + t // tm, 0)),
                  pl.BlockSpec((tm, LANES), row), pl.BlockSpec((1, d), lambda i: (0, 0))],
        out_specs=pl.BlockSpec((tm, d), row),
        out_shape=jax.ShapeDtypeStruct((t, d), F32),
        compiler_params=_cparams("parallel"),
    )(x2, y, y, route, norm_final.reshape(1, d))


def _moe_plan(route, n_experts, tile):
    t = route.shape[0]
    pairs = MOE_TOPK * t
    n_tiles = pairs // tile + n_experts
    e_flat = route[:, :MOE_TOPK].astype(I32).reshape(pairs)
    order = jnp.argsort(e_flat, stable=True).astype(I32)
    counts = jnp.sum((e_flat[:, None] == jnp.arange(n_experts, dtype=I32)[None, :]).astype(I32), axis=0)
    tiles_per = (counts + tile - 1) // tile
    tile_end = jnp.cumsum(tiles_per)
    sorted_start = jnp.cumsum(counts) - counts
    tile_ids = jnp.arange(n_tiles, dtype=I32)
    tile_valid = (tile_ids < tile_end[-1]).astype(I32)
    tile_expert = jnp.minimum(jnp.sum((tile_ids[:, None] >= tile_end[None, :]).astype(I32), axis=1),
                              n_experts - 1)
    last_expert = tile_expert[jnp.maximum(tile_end[-1] - 1, 0)]
    tile_expert = jnp.where(tile_valid > 0, tile_expert, last_expert)
    tile_first = (tile_end - tiles_per)[tile_expert]
    lane_row = jnp.arange(tile, dtype=I32)[None, :]
    row_in_group = ((tile_ids - tile_first) * tile)[:, None] + lane_row
    row_valid = (row_in_group < counts[tile_expert][:, None]) & (tile_valid[:, None] > 0)
    src = jnp.clip(sorted_start[tile_expert][:, None] + row_in_group, 0, pairs - 1)
    pair = order[src]
    token = jnp.where(row_valid, pair // MOE_TOPK, 0)
    spare = pairs + (tile_ids % 2)[:, None] * tile + lane_row
    dest = jnp.where(row_valid, (pair % MOE_TOPK) * t + pair // MOE_TOPK, spare)
    return tile_expert, tile_valid, token.reshape(-1), dest.reshape(-1)


def kernel(x, mem, norm_mix, w_in, conv_w, conv_b, dt_bias, a_log, d_skip, ssm_norm, w_ssm_out,
           w_att_out, w_mix_out, norm_x, norm_mem, w_xq, w_xkv, w_xo, norm_moe, w_rg, b_rg, w_re,
           b_re, w_e_gate, w_e_up, w_e_down, norm_final):
    batch, seq, d = x.shape
    mem_len = mem.shape[1]
    t = batch * seq
    heads = dt_bias.shape[1]
    d_inner = heads * SSM_HEAD_DIM
    conv_dim = conv_w.shape[1]
    att_w = w_att_out.shape[1]
    att_heads = att_w // ATT_HEAD_DIM
    n_experts = w_re.shape[2]
    groups = w_rg.shape[2]
    per_group = n_experts // groups
    idx_heads = (w_in.shape[2] - (d_inner + conv_dim + heads + att_w + 2 * ATT_HEAD_DIM
                                  + IDX_HEAD_DIM + 2 * d)) // (IDX_HEAD_DIM + 1)
    idx_w = idx_heads * IDX_HEAD_DIM
    assert w_in.shape[0] == 1, "one layer; the final norm is fused into the layer's last call"
    assert seq % SSM_CHUNK == 0 and seq % Q_BLOCK == 0 and heads + idx_heads <= LANES
    assert groups + n_experts <= LANES and t % 1024 == 0 and heads % (2 * SSM_GROUPS) == 0
    li = 0

    sizes = (d_inner, conv_dim, heads, att_w, ATT_HEAD_DIM, ATT_HEAD_DIM, idx_w, IDX_HEAD_DIM,
             idx_heads, d, d)
    offs = [0]
    for s in sizes:
        offs.append(offs[-1] + s)
    wi_ = w_in[li]
    col = lambda k: wi_[:, offs[k]:offs[k + 1]]
    pad = lambda a, n: jnp.pad(a, ((0, 0), (0, n - a.shape[1])))
    lpad = lambda a, n: jnp.pad(a, ((0, 0), (n - a.shape[1], 0)))
    both = lambda a: [pad(a, LANES), lpad(a, LANES)]
    w_z = col(0).astype(BF16)
    w_xbc = col(1).astype(BF16)
    w_small = pad(jnp.concatenate([col(2), col(8)], axis=1), LANES).astype(BF16)
    w_qq = jnp.concatenate([col(3) * (LOG2E * ATT_HEAD_DIM ** -0.5), col(6)], axis=1).astype(BF16)
    w_kv = jnp.concatenate(both(col(7)) + [pad(col(4), LANES), pad(col(5), LANES)],
                           axis=1).astype(BF16)
    w_gates = jnp.concatenate([col(9), col(10)], axis=1).astype(BF16)

    h = x.reshape(t, d)
    u = _rmsnorm(h, norm_mix[li], BF16, 512)
    z = _matmul(u, w_z, BF16, 1024, 1024)
    xbc = _matmul(u, w_xbc, BF16, 1024, 1024)
    small = _matmul(u, w_small, F32, 512, LANES)
    qq = _matmul(u, w_qq, BF16, 512, att_w + idx_w)
    kvp = _matmul(u, w_kv, BF16, 512, 4 * LANES)
    gates = _matmul(u, w_gates, BF16, 1024, 1024)

    dtt = jnp.swapaxes(small[:, :heads].reshape(batch, seq, heads), 1, 2)
    ys = _ssd(z, xbc, small, dtt, conv_w[li], conv_b[li], dt_bias[li], a_log[li], d_skip[li],
              ssm_norm[li], batch, seq)
    wit = jnp.swapaxes(small[:, heads:heads + idx_heads].reshape(batch, seq, idx_heads), 1, 2)
    oa = _dsa(qq, kvp, wit, batch, seq, att_heads, idx_heads)

    x1, xn1 = _merge(h, ys, oa, gates, w_ssm_out[li].astype(BF16), w_att_out[li].astype(BF16),
                     w_mix_out[li].astype(BF16), norm_x[li], 512)

    mn = _rmsnorm(mem.reshape(batch * mem_len, d), norm_mem[li], BF16, 512)
    kvm = _matmul(mn, w_xkv[li].astype(BF16), BF16, 512, 1024)
    x2 = _xattn(x1, xn1, kvm, w_xq[li].astype(BF16), w_xo[li].astype(BF16), batch, seq,
                mem_len, 512)

    wr = pad(jnp.concatenate([w_rg[li], w_re[li]], axis=1), LANES)
    br = pad(jnp.concatenate([b_rg[li], b_re[li]]).reshape(1, -1), LANES)
    xm, route = _router(x2, norm_moe[li], wr, br, groups, per_group, 512)

    tile_expert, tile_valid, token, dest = _moe_plan(route, n_experts, EXPERT_TILE)
    ye = _experts(xm, tile_expert, tile_valid, token, dest, w_e_gate[li].astype(BF16),
                  w_e_up[li].astype(BF16), w_e_down[li].astype(BF16),
                  MOE_TOPK * t + 2 * EXPERT_TILE)
    out = _combine(x2, ye, route, norm_final, 512)
    return out.reshape(batch, seq, d)
```

```python
import functools
import math

import numpy as np
import jax
import jax.numpy as jnp
from jax import lax
from jax.experimental import pallas as pl
from jax.experimental.pallas import tpu as pltpu

F32 = jnp.float32
BF16 = jnp.bfloat16
I32 = jnp.int32
EPS = 1e-6
HIGHEST = lax.Precision.HIGHEST
LOG2E = math.log2(math.e)

SSM_HEAD_DIM = 64
SSM_GROUPS = 4
SSM_STATE = 128
SSM_CHUNK = 128
ATT_HEAD_DIM = 64
IDX_HEAD_DIM = 64
TOPK_MAX = 256
Q_BLOCK = 128
NEG_INF = -1e30
X_HEADS = 4
MOE_GROUPS = 4
MOE_TOPK = 2

LANES = 128
VMEM_LIMIT = 56 * 1024 * 1024
EXPERT_TILE = 256
DSA_WIDTHS = 8
DSA_HEADS_PER_DOT = 4


def _cparams(*sem):
    return pltpu.CompilerParams(dimension_semantics=sem, vmem_limit_bytes=VMEM_LIMIT)


def _sigmoid(x):
    return 1.0 / (1.0 + jnp.exp(-x))


def _softplus(x):
    return jnp.maximum(x, 0.0) + jnp.log1p(jnp.exp(-jnp.abs(x)))


def _dot(a, b, precision=None):
    return jnp.dot(a, b, preferred_element_type=F32, precision=precision)


def _split3(x):
    p0 = x.astype(BF16)
    r1 = x - p0.astype(F32)
    p1 = r1.astype(BF16)
    p2 = (r1 - p1.astype(F32)).astype(BF16)
    return p0, p1, p2


def _dot_exact_rhs(x, m):
    mb = m.astype(BF16)
    p0, p1, p2 = _split3(x)
    return _dot(p0, mb) + _dot(p1, mb) + _dot(p2, mb)


def _dot_exact_lhs(m, x):
    mb = m.astype(BF16)
    p0, p1, p2 = _split3(x)
    return _dot(mb, p0) + _dot(mb, p1) + _dot(mb, p2)


def _dot_nt(a, b):
    return lax.dot_general(a, b, (((1,), (1,)), ((), ())), preferred_element_type=F32)


def _reduce_rows(x, op, pair_op, chains=8):
    w, n = x.shape
    while w % (8 * chains):
        chains //= 2
    step = w // chains
    parts = [op(x[i * step:(i + 1) * step].reshape(step // 8, 8, n), axis=0) for i in range(chains)]
    while len(parts) > 1:
        parts = [pair_op(parts[i], parts[i + 1]) for i in range(0, len(parts), 2)]
    return op(parts[0], axis=0, keepdims=True)


def _rmsnorm_kernel(x_ref, g_ref, o_ref):
    x = x_ref[...]
    y = x * lax.rsqrt(jnp.mean(x * x, axis=-1, keepdims=True) + EPS) * g_ref[...]
    o_ref[...] = y.astype(o_ref.dtype)


def _rmsnorm(x, g, out_dtype, tm):
    m, d = x.shape
    return pl.pallas_call(
        _rmsnorm_kernel,
        grid=(m // tm,),
        in_specs=[pl.BlockSpec((tm, d), lambda i: (i, 0)),
                  pl.BlockSpec((1, d), lambda i: (0, 0))],
        out_specs=pl.BlockSpec((tm, d), lambda i: (i, 0)),
        out_shape=jax.ShapeDtypeStruct((m, d), out_dtype),
        compiler_params=_cparams("parallel"),
    )(x, g.reshape(1, d))


def _mm_kernel(a_ref, b_ref, o_ref):
    o_ref[...] = _dot(a_ref[...], b_ref[...]).astype(o_ref.dtype)


def _matmul(a, b, out_dtype, tm, tn):
    m, k = a.shape
    n = b.shape[1]
    return pl.pallas_call(
        _mm_kernel,
        grid=(n // tn, m // tm),
        in_specs=[pl.BlockSpec((tm, k), lambda j, i: (i, 0)),
                  pl.BlockSpec((k, tn), lambda j, i: (0, j))],
        out_specs=pl.BlockSpec((tm, tn), lambda j, i: (i, j)),
        out_shape=jax.ShapeDtypeStruct((m, n), out_dtype),
        compiler_params=_cparams("parallel", "parallel"),
    )(a, b)


def _ssd_kernel(z_ref, xbc_ref, dt_ref, dtt_ref, cw_ref, cb_ref, dtb_ref, dtbt_ref,
                alog_ref, alogt_ref, dskip_ref, norm_ref, e_ref, o_ref,
                tail_ref, st_ref, y_ref, *, heads, d_inner):
    q = SSM_CHUNK
    n = SSM_STATE
    hpg = heads // SSM_GROUPS
    gw = hpg * SSM_HEAD_DIM
    c = pl.program_id(1)

    @pl.when(c == 0)
    def _():
        tail_ref[...] = jnp.zeros_like(tail_ref)
        st_ref[...] = jnp.zeros_like(st_ref)

    x = xbc_ref[...]
    tl = tail_ref.shape[0]
    xcat = jnp.concatenate([tail_ref[...], x], axis=0)
    tail_ref[...] = x[q - tl:, :]
    cw = cw_ref[...]
    kconv = cw.shape[0]
    srow = lax.broadcasted_iota(I32, ((kconv - 1) * q, tl + q), 0)
    scol = lax.broadcasted_iota(I32, ((kconv - 1) * q, tl + q), 1)
    shift = jnp.where(scol - (tl - (kconv - 1)) == srow - (q - 1) * (srow // q), 1.0, 0.0)
    shifted = _dot(shift.astype(xcat.dtype), xcat)
    acc = cb_ref[...] + cw[kconv - 1:kconv, :] * x.astype(F32)
    for k in range(kconv - 1):
        acc = acc + cw[k:k + 1, :] * shifted[k * q:(k + 1) * q, :]
    xc = acc * _sigmoid(acc)
    xs = xc[:, :d_inner]
    bm = xc[:, d_inner:d_inner + SSM_GROUPS * n]
    cm = xc[:, d_inner + SSM_GROUPS * n:]

    dt = _softplus(dt_ref[...] + dtb_ref[...])
    da = dt * (-jnp.exp(alog_ref[...]))
    dtt = _softplus(dtt_ref[...] + dtbt_ref[...])
    dat = dtt * (-jnp.exp(alogt_ref[...]))
    rows = lax.broadcasted_iota(I32, (q, q), 0)
    cols = lax.broadcasted_iota(I32, (q, q), 1)
    causal = rows >= cols
    tril = jnp.where(causal, 1.0, 0.0)
    triu = jnp.where(rows <= cols, 1.0, 0.0)
    a_cs = _dot_exact_lhs(tril, da)
    a_cst = _dot_exact_rhs(dat, triu)
    expand = e_ref[...]
    dt_e = _dot_exact_rhs(dt, expand)
    acs_e = _dot_exact_rhs(a_cs, expand)
    expa = jnp.exp(acs_e)
    a_last = acs_e[q - 1:q, :]
    xdt = xs * dt_e
    xdt_b = xdt.astype(BF16)
    xdec_b = (xdt * jnp.exp(a_last - acs_e)).astype(BF16)
    st = st_ref[...]
    st_b = st.astype(BF16)
    lane = lax.broadcasted_iota(I32, (q, 2 * SSM_HEAD_DIM), 1)

    for g in range(SSM_GROUPS):
        bg = bm[:, g * n:(g + 1) * n]
        cg = cm[:, g * n:(g + 1) * n].astype(BF16)
        cb = _dot_nt(cg, bg.astype(BF16))
        gs = slice(g * gw, (g + 1) * gw)
        y_off = _dot(cg, st_b[:, gs])
        s_new = _dot(bg.T.astype(BF16), xdec_b[:, gs])
        st_ref[:, gs] = st[:, gs] * expa[q - 1:q, gs] + s_new
        for j in range(hpg // 2):
            h0 = g * hpg + 2 * j
            c0 = h0 * SSM_HEAD_DIM
            xp = xdt_b[:, c0:c0 + 2 * SSM_HEAD_DIM]
            parts = []
            for h in (h0, h0 + 1):
                seg = a_cs[:, h:h + 1] - a_cst[h:h + 1, :]
                lmat = jnp.exp(jnp.where(causal, seg, -jnp.inf))
                parts.append(_dot((cb * lmat).astype(BF16), xp))
            y_diag = jnp.where(lane < SSM_HEAD_DIM, parts[0], parts[1])
            cs = slice(c0, c0 + 2 * SSM_HEAD_DIM)
            y_ref[:, cs] = y_diag + y_off[:, c0 - g * gw:c0 - g * gw + 2 * SSM_HEAD_DIM] * expa[:, cs]

    y = y_ref[...] + dskip_ref[...] * xs
    zz = z_ref[...].astype(F32)
    yg = y * (zz * _sigmoid(zz))
    out = yg * lax.rsqrt(jnp.mean(yg * yg, axis=-1, keepdims=True) + EPS) * norm_ref[...]
    o_ref[...] = out.astype(o_ref.dtype)


def _ssd(z, xbc, small, dtt, conv_w, conv_b, dt_bias, a_log, d_skip, ssm_norm, batch, seq):
    heads = dt_bias.shape[0]
    d_inner = heads * SSM_HEAD_DIM
    conv_dim = xbc.shape[1]
    q = SSM_CHUNK
    nc = seq // q
    expand = jnp.repeat(jnp.eye(LANES, heads, dtype=BF16), SSM_HEAD_DIM, axis=1)
    lane_pad = lambda v: jnp.pad(v.reshape(1, -1), ((0, 0), (0, LANES - heads)))
    row = lambda b, c: (b * nc + c, 0)
    const = lambda b, c: (0, 0)
    kern = functools.partial(_ssd_kernel, heads=heads, d_inner=d_inner)
    return pl.pallas_call(
        kern,
        grid=(batch, nc),
        in_specs=[
            pl.BlockSpec((q, d_inner), row),
            pl.BlockSpec((q, conv_dim), row),
            pl.BlockSpec((q, LANES), row),
            pl.BlockSpec((None, heads, q), lambda b, c: (b, 0, c)),
            pl.BlockSpec(conv_w.T.shape, const),
            pl.BlockSpec((1, conv_dim), const),
            pl.BlockSpec((1, LANES), const),
            pl.BlockSpec((heads, 1), const),
            pl.BlockSpec((1, LANES), const),
            pl.BlockSpec((heads, 1), const),
            pl.BlockSpec((1, d_inner), const),
            pl.BlockSpec((1, d_inner), const),
            pl.BlockSpec((LANES, d_inner), const),
        ],
        out_specs=pl.BlockSpec((q, d_inner), row),
        out_shape=jax.ShapeDtypeStruct((batch * seq, d_inner), BF16),
        scratch_shapes=[pltpu.VMEM((16, conv_dim), xbc.dtype),
                        pltpu.VMEM((SSM_STATE, d_inner), F32),
                        pltpu.VMEM((q, d_inner), F32)],
        compiler_params=_cparams("parallel", "arbitrary"),
    )(z, xbc, small, dtt, conv_w.T, conv_b.reshape(1, -1),
      lane_pad(dt_bias), dt_bias.reshape(-1, 1), lane_pad(a_log), a_log.reshape(-1, 1),
      jnp.repeat(d_skip, SSM_HEAD_DIM).reshape(1, -1), ssm_norm.reshape(1, -1), expand)


def _order_key(x):
    bits = int(np.array(x, np.float32).view(np.int32))
    return bits ^ ((bits >> 31) & 0x7FFFFFFF)


def _dsa_consts(seq, att_heads):
    hd = ATT_HEAD_DIM
    assert seq <= 16 * 256 and 2 * hd == LANES
    qc = np.zeros((att_heads, LANES), np.float32)
    for h in range(att_heads):
        rest = np.float32(2.0 ** (-8.0 * (h + 1) / att_heads) * LOG2E)
        for i in range(3):
            piece = np.float32(rest.astype(jnp.bfloat16))
            rest = np.float32(rest - piece)
            qc[h, hd + i] = 16.0 * piece
            qc[h, hd + 3 + i] = piece
    pos = np.arange(seq)
    kpos = np.zeros((seq, LANES), np.float32)
    kpos[:, hd:hd + 3] = (pos // 16)[:, None]
    kpos[:, hd + 3:hd + 6] = (pos % 16)[:, None]
    return jnp.asarray(qc), jnp.asarray(kpos, dtype=BF16)


def _dsa_kernel(q_ref, qi_ref, ka_ref, kip_ref, vt_ref, wit_ref, qc_ref, o_ref,
                key_ref, mb_ref, qa_ref, *, topk, att_heads, idx_heads, widths):
    qb = Q_BLOCK
    seq = ka_ref.shape[0]
    nq = seq // qb
    blk = pl.program_id(1)
    hd = ATT_HEAD_DIM
    kf = float(topk)
    key_ni = _order_key(NEG_INF)
    left = lax.broadcasted_iota(I32, (qb, LANES), 1) < hd
    tpos = blk * qb + lax.broadcasted_iota(I32, (1, qb), 1)

    hpd = DSA_HEADS_PER_DOT
    for h in range(att_heads):
        qpair = q_ref[:, (h // 2) * LANES:(h // 2 + 1) * LANES].astype(F32)
        qh = qpair if h % 2 == 0 else pltpu.roll(qpair, hd, axis=1)
        qa_ref[h // hpd, (h % hpd) * qb:(h % hpd + 1) * qb, :] = jnp.where(
            left, qh, qc_ref[h:h + 1, :]).astype(BF16)

    def body(w):
        n_out = float(seq - w)
        spos = lax.broadcasted_iota(I32, (w, 1), 0)
        causal = spos <= tpos

        wit = wit_ref[...] * (idx_heads ** -0.5 * IDX_HEAD_DIM ** -0.5)
        isc = jnp.zeros((w, qb), F32)
        qi_rows = jnp.concatenate([qi_ref[:, g * LANES:(g + 1) * LANES]
                                   for g in range(idx_heads // 2)], axis=0)
        for par in range(2):
            rel = _dot_nt(kip_ref[:w, par * LANES:(par + 1) * LANES], qi_rows)
            for g in range(idx_heads // 2):
                h = 2 * g + par
                isc = isc + jnp.maximum(rel[:, g * qb:(g + 1) * qb], 0.0) * wit[h:h + 1, :]
        masked = jnp.where(causal, isc, NEG_INF) + 0.0
        bits = pltpu.bitcast(masked, I32)
        key_ref[:w, :] = bits ^ ((bits >> 31) & 0x7FFFFFFF)

        def count(mask):
            return _reduce_rows(jnp.where(mask, 1.0, 0.0), jnp.sum, jnp.add)

        def count_ge(cand):
            return count(key_ref[:w, :] >= cand) + jnp.where(key_ni >= cand, n_out, 0.0)

        int_min = jnp.full((1, qb), -2 ** 31, I32)
        thr0 = jnp.where(count_ge(jnp.zeros((1, qb), I32)) >= kf, 0, int_min)

        def thr_body(j, thr):
            cand = thr + lax.shift_left(jnp.int32(1), 30 - j)
            return jnp.where(count_ge(cand) >= kf, cand, thr)

        thr = lax.fori_loop(0, 31, thr_body, thr0)

        key = key_ref[:w, :]
        gt = key > thr
        eqc = (key == thr) & causal
        need = kf - count(gt) - jnp.where(key_ni > thr, n_out, 0.0)
        has_tie = jnp.max(jnp.where(count(eqc) > need, 1.0, 0.0)) > 0.0

        @pl.when(jnp.logical_not(has_tie))
        def _():
            mb_ref[:w, :] = jnp.where(causal & (key >= thr), 0.0, -jnp.inf)

        @pl.when(has_tie)
        def _():
            cut = jnp.zeros((1, qb), I32)
            for b in range(max(1, (w - 1).bit_length()) - 1, -1, -1):
                cand = cut + (1 << b)
                cut = jnp.where(count(eqc & (spos < cand)) < need, cand, cut)
            sel = gt | (eqc & (spos <= cut))
            mb_ref[:w, :] = jnp.where(causal & sel, 0.0, -jnp.inf)

        n_dots = att_heads // hpd
        s_next = _dot_nt(ka_ref[:w, :], qa_ref[0])
        for jd in range(n_dots):
            s = s_next
            if jd + 1 < n_dots:
                s_next = _dot_nt(ka_ref[:w, :], qa_ref[jd + 1])
            ps = []
            for e in range(hpd):
                sh = s[:, e * qb:(e + 1) * qb] + mb_ref[:w, :]
                ps.append(jnp.exp2((sh - _reduce_rows(sh, jnp.max, jnp.maximum)).astype(BF16)))
            acc = _dot(vt_ref[:, :w], jnp.concatenate(ps, axis=1))
            o = acc * (1.0 / acc[hd:hd + 1, :])
            for e in range(0, hpd, 2):
                pair = jnp.where(left, o[:, e * qb:(e + 1) * qb].T,
                                 pltpu.roll(o[:, (e + 1) * qb:(e + 2) * qb].T, hd, axis=1))
                g = (jd * hpd + e) // 2
                o_ref[:, g * LANES:(g + 1) * LANES] = pair.astype(o_ref.dtype)

    per = nq // widths
    for v in range(widths):
        @pl.when((blk >= v * per) & (blk < (v + 1) * per))
        def _(v=v):
            body((v + 1) * per * qb)


def _dsa(qq, kvp, wit, batch, seq, att_heads, idx_heads):
    qb = Q_BLOCK
    nb = seq // qb
    hd = ATT_HEAD_DIM
    att_w = att_heads * hd
    idx_w = idx_heads * IDX_HEAD_DIM
    widths = min(DSA_WIDTHS, nb)
    assert att_w % idx_w == 0 and nb % widths == 0 and idx_heads % 2 == 0
    assert att_heads % DSA_HEADS_PER_DOT == 0 and DSA_HEADS_PER_DOT % 2 == 0
    assert 2 * IDX_HEAD_DIM == LANES and qb == LANES
    topk = min(TOPK_MAX, seq // 4)
    qc, kpos = _dsa_consts(seq, att_heads)
    ka = kvp[:, 2 * LANES:3 * LANES] + jnp.tile(kpos, (batch, 1))
    vt = jnp.swapaxes(kvp[:, 3 * LANES:].reshape(batch, seq, LANES), 1, 2)
    vt = vt.at[:, hd, :].set(1.0)
    kern = functools.partial(_dsa_kernel, topk=topk, att_heads=att_heads, idx_heads=idx_heads,
                             widths=widths)
    return pl.pallas_call(
        kern,
        grid=(batch, nb),
        in_specs=[
            pl.BlockSpec((qb, att_w), lambda b, i: (b * nb + i, 0)),
            pl.BlockSpec((qb, idx_w), lambda b, i: (b * nb + i, att_w // idx_w)),
            pl.BlockSpec((seq, LANES), lambda b, i: (b, 0)),
            pl.BlockSpec((seq, 2 * LANES), lambda b, i: (b, 0)),
            pl.BlockSpec((None, LANES, seq), lambda b, i: (b, 0, 0)),
            pl.BlockSpec((None, idx_heads, qb), lambda b, i: (b, 0, i)),
            pl.BlockSpec(qc.shape, lambda b, i: (0, 0)),
        ],
        out_specs=pl.BlockSpec((qb, att_w), lambda b, i: (b * nb + i, 0)),
        out_shape=jax.ShapeDtypeStruct((batch * seq, att_w), BF16),
        scratch_shapes=[pltpu.VMEM((seq, qb), I32),
                        pltpu.VMEM((seq, qb), F32),
                        pltpu.VMEM((att_heads // DSA_HEADS_PER_DOT, DSA_HEADS_PER_DOT * qb, LANES),
                                   BF16)],
        compiler_params=_cparams("parallel", "parallel"),
    )(qq, qq, ka, kvp, vt, wit, qc)


def _merge_kernel(x_ref, ys_ref, oa_ref, g_ref, wso_ref, wao_ref, wmo_ref, nx_ref,
                  x1_ref, xn_ref):
    d = x_ref.shape[1]
    y_ssm = _dot(ys_ref[...], wso_ref[...])
    y_att = _dot(oa_ref[...], wao_ref[...])
    g = g_ref[...].astype(F32)
    mix = _sigmoid(g[:, :d]) * y_ssm + _sigmoid(g[:, d:]) * y_att
    x1 = x_ref[...] + _dot(mix.astype(BF16), wmo_ref[...])
    x1_ref[...] = x1
    xn = x1 * lax.rsqrt(jnp.mean(x1 * x1, axis=-1, keepdims=True) + EPS) * nx_ref[...]
    xn_ref[...] = xn.astype(xn_ref.dtype)


def _merge(x, ys, oa, gates, wso, wao, wmo, norm_x, tm):
    t, d = x.shape
    row = lambda i: (i, 0)
    const = lambda i: (0, 0)
    return pl.pallas_call(
        _merge_kernel,
        grid=(t // tm,),
        in_specs=[pl.BlockSpec((tm, d), row),
                  pl.BlockSpec((tm, ys.shape[1]), row),
                  pl.BlockSpec((tm, oa.shape[1]), row),
                  pl.BlockSpec((tm, 2 * d), row),
                  pl.BlockSpec(wso.shape, const),
                  pl.BlockSpec(wao.shape, const),
                  pl.BlockSpec(wmo.shape, const),
                  pl.BlockSpec((1, d), const)],
        out_specs=[pl.BlockSpec((tm, d), row), pl.BlockSpec((tm, d), row)],
        out_shape=[jax.ShapeDtypeStruct((t, d), F32), jax.ShapeDtypeStruct((t, d), BF16)],
        compiler_params=_cparams("parallel"),
    )(x, ys, oa, gates, wso, wao, wmo, norm_x.reshape(1, d))


def _xattn_kernel(x1_ref, xn_ref, kv_ref, wq_ref, wo_ref, x2_ref):
    d = x1_ref.shape[1]
    hd = d // X_HEADS
    qf = _dot(xn_ref[...], wq_ref[...]) * (hd ** -0.5)
    qb = qf.astype(BF16)
    outs = []
    for h in range(X_HEADS):
        kh = kv_ref[:, h * hd:(h + 1) * hd]
        vh = kv_ref[:, d + h * hd:d + (h + 1) * hd]
        s = _dot_nt(qb[:, h * hd:(h + 1) * hd], kh)
        m = jnp.max(s, axis=-1, keepdims=True)
        p = jnp.exp(s - m)
        l = jnp.sum(p, axis=-1, keepdims=True)
        outs.append((_dot(p.astype(BF16), vh) / l).astype(BF16))
    o = jnp.concatenate(outs, axis=-1)
    x2_ref[...] = x1_ref[...] + _dot(o, wo_ref[...])


def _xattn(x1, xn, kvm, wq, wo, batch, seq, mem_len, tm):
    t, d = x1.shape
    nt = seq // tm
    row = lambda b, i: (b * nt + i, 0)
    const = lambda b, i: (0, 0)
    return pl.pallas_call(
        _xattn_kernel,
        grid=(batch, nt),
        in_specs=[pl.BlockSpec((tm, d), row),
                  pl.BlockSpec((tm, d), row),
                  pl.BlockSpec((mem_len, 2 * d), lambda b, i: (b, 0)),
                  pl.BlockSpec(wq.shape, const),
                  pl.BlockSpec(wo.shape, const)],
        out_specs=pl.BlockSpec((tm, d), row),
        out_shape=jax.ShapeDtypeStruct((t, d), F32),
        compiler_params=_cparams("parallel", "parallel"),
    )(x1, xn, kvm, wq, wo)


def _router_kernel(x_ref, g_ref, wr_ref, br_ref, xm_ref, route_ref, *, groups, per_group):
    x = x_ref[...]
    xm = x * lax.rsqrt(jnp.mean(x * x, axis=-1, keepdims=True) + EPS) * g_ref[...]
    xm_ref[...] = xm
    logits = _dot(xm, wr_ref[...], HIGHEST) + br_ref[...]
    lane = lax.broadcasted_iota(I32, logits.shape, 1)
    lanef = lane.astype(F32)
    far = float(LANES)
    ninf = -jnp.inf

    gl = jnp.where(lane < groups, logits, ninf)
    gmax = jnp.max(gl, axis=-1, keepdims=True)
    gsel = jnp.min(jnp.where(gl == gmax, lanef, far), axis=-1, keepdims=True)
    ge = jnp.exp(gl - gmax)
    pg = jnp.max(ge / jnp.sum(ge, axis=-1, keepdims=True), axis=-1, keepdims=True)

    lo = groups + per_group * gsel
    em = (lanef >= lo) & (lanef < lo + per_group)
    el = jnp.where(em, logits, ninf)
    ee = jnp.exp(el - jnp.max(el, axis=-1, keepdims=True))
    ep = jnp.where(em, ee / jnp.sum(ee, axis=-1, keepdims=True), -1.0)
    p1 = jnp.max(ep, axis=-1, keepdims=True)
    i1 = jnp.min(jnp.where(ep == p1, lanef, far), axis=-1, keepdims=True)
    ep2 = jnp.where(lanef == i1, -1.0, ep)
    p2 = jnp.max(ep2, axis=-1, keepdims=True)
    i2 = jnp.min(jnp.where(ep2 == p2, lanef, far), axis=-1, keepdims=True)
    w1 = pg * p1 / (p1 + p2)
    w2 = pg * p2 / (p1 + p2)
    route_ref[...] = jnp.where(lane == 0, i1 - groups,
                     jnp.where(lane == 1, i2 - groups,
                     jnp.where(lane == 2, w1,
                     jnp.where(lane == 3, w2, 0.0))))


def _router(x2, norm_moe, wr, br, groups, per_group, tm):
    t, d = x2.shape
    row = lambda i: (i, 0)
    const = lambda i: (0, 0)
    kern = functools.partial(_router_kernel, groups=groups, per_group=per_group)
    return pl.pallas_call(
        kern,
        grid=(t // tm,),
        in_specs=[pl.BlockSpec((tm, d), row), pl.BlockSpec((1, d), const),
                  pl.BlockSpec((d, LANES), const), pl.BlockSpec((1, LANES), const)],
        out_specs=[pl.BlockSpec((tm, d), row), pl.BlockSpec((tm, LANES), row)],
        out_shape=[jax.ShapeDtypeStruct((t, d), F32), jax.ShapeDtypeStruct((t, LANES), F32)],
        compiler_params=_cparams("parallel"),
    )(x2, norm_moe.reshape(1, d), wr, br)


def _expert_kernel(te_ref, tv_ref, gcur_ref, gnext_ref, sprev_ref, sidx_ref, xm_ref, wg_ref, wu_ref,
                   wd_ref, y_ref, xbuf, ybuf, gsem, ssem):
    i = pl.program_id(0)
    nt = pl.num_programs(0)
    slot = i % 2
    groups8 = xbuf.shape[1]
    rows = 8 * groups8
    d = xbuf.shape[3]
    nxt = jnp.minimum(i + 1, nt - 1)
    valid = tv_ref[i] > 0
    has_next = (i + 1 < nt) & (tv_ref[nxt] > 0)

    def gather_copy(src_row, s, g8, u):
        return pltpu.make_async_copy(xm_ref.at[pl.ds(src_row, 1), :],
                                     xbuf.at[s, g8, pl.ds(u, 1), :], gsem.at[s])

    def scatter_copy(s, g8, u, dst_row):
        return pltpu.make_async_copy(ybuf.at[s, g8, pl.ds(u, 1), :],
                                     y_ref.at[pl.ds(dst_row, 1), :], ssem.at[s])

    def per_row(fn):
        def g8_body(g8, carry):
            for u in range(8):
                fn(g8, u)
            return carry
        lax.fori_loop(0, groups8, g8_body, 0)

    def gather_start(idx_ref, s):
        per_row(lambda g8, u: gather_copy(idx_ref[0, g8 * 8 + u], s, g8, u).start(priority=u % 2))

    def gather_wait(s):
        per_row(lambda g8, u: gather_copy(0, s, g8, u).wait())

    def scatter_start(idx_ref, s):
        per_row(lambda g8, u: scatter_copy(s, g8, u, idx_ref[0, g8 * 8 + u]).start(priority=u % 2))

    def scatter_wait(s):
        per_row(lambda g8, u: scatter_copy(s, g8, u, 0).wait())

    @pl.when(i == 0)
    def _():
        ybuf[1] = jnp.zeros(ybuf.shape[1:], ybuf.dtype)
        n_real = y_ref.shape[0] - 2 * rows

        def fill_copy(g):
            return pltpu.make_async_copy(ybuf.at[1, g % groups8],
                                         y_ref.at[pl.ds(n_real + g * 8, 8), :], ssem.at[1])

        def fill_start(g, carry):
            fill_copy(g).start()
            return carry

        def fill_wait(g, carry):
            fill_copy(g).wait()
            return carry

        lax.fori_loop(0, 2 * groups8, fill_start, 0)
        lax.fori_loop(0, 2 * groups8, fill_wait, 0)

    @pl.when((i == 0) & valid)
    def _():
        gather_start(gcur_ref, 0)

    def compute():
        xb = xbuf[slot].reshape(rows, d).astype(BF16)
        gate = _dot(xb, wg_ref[...])
        hid = gate * _sigmoid(gate) * _dot(xb, wu_ref[...])
        ybuf[slot] = _dot(hid.astype(BF16), wd_ref[...]).reshape(groups8, 8, d)

    def issue_inline(with_prev_scatter):
        for r in range(rows):
            g8, u = divmod(r, 8)
            gather_copy(gnext_ref[0, r], 1 - slot, g8, u).start(priority=r % 2)
            if with_prev_scatter:
                scatter_copy(1 - slot, g8, u, sprev_ref[0, r]).start(priority=(r + 1) % 2)

    @pl.when(valid)
    def _():
        gather_wait(slot)

        @pl.when(i >= 2)
        def _():
            scatter_wait(slot)

        @pl.when(has_next & (i == 0))
        def _():
            compute()
            issue_inline(False)

        @pl.when(has_next & (i >= 1))
        def _():
            compute()
            issue_inline(True)

        @pl.when(jnp.logical_not(has_next))
        def _():
            compute()

            @pl.when(i >= 1)
            def _():
                scatter_start(sprev_ref, 1 - slot)

            scatter_start(sidx_ref, slot)

            @pl.when(i >= 1)
            def _():
                scatter_wait(1 - slot)

            scatter_wait(slot)


def _experts(xm, tile_expert, tile_valid, gidx, sidx, wg, wu, wd, n_out):
    t, d = xm.shape
    tm = EXPERT_TILE
    nt = tile_expert.shape[0]
    hid = wg.shape[2]
    smem_rows = lambda imap: pl.BlockSpec((None, 1, tm), imap, memory_space=pltpu.SMEM)
    wspec = lambda shape: pl.BlockSpec((None,) + shape, lambda i, te, tv: (te[i], 0, 0))
    grid_spec = pltpu.PrefetchScalarGridSpec(
        num_scalar_prefetch=2,
        grid=(nt,),
        in_specs=[smem_rows(lambda i, te, tv: (i, 0, 0)),
                  smem_rows(lambda i, te, tv: (jnp.minimum(i + 1, nt - 1), 0, 0)),
                  smem_rows(lambda i, te, tv: (jnp.maximum(i - 1, 0), 0, 0)),
                  smem_rows(lambda i, te, tv: (i, 0, 0)),
                  pl.BlockSpec(memory_space=pl.ANY),
                  wspec((d, hid)), wspec((d, hid)), wspec((hid, d))],
        out_specs=pl.BlockSpec(memory_space=pl.ANY),
        scratch_shapes=[pltpu.VMEM((2, tm // 8, 8, d), F32), pltpu.VMEM((2, tm // 8, 8, d), F32),
                        pltpu.SemaphoreType.DMA((2,)), pltpu.SemaphoreType.DMA((2,))],
    )
    g3 = gidx.reshape(nt, 1, tm)
    s3 = sidx.reshape(nt, 1, tm)
    return pl.pallas_call(
        _expert_kernel,
        grid_spec=grid_spec,
        out_shape=jax.ShapeDtypeStruct((n_out, d), F32),
        compiler_params=_cparams("arbitrary"),
    )(tile_expert, tile_valid, g3, g3, s3, s3, xm, wg, wu, wd)


def _combine_kernel(x_ref, y1_ref, y2_ref, route_ref, g_ref, o_ref):
    w1 = route_ref[:, 2:3]
    w2 = route_ref[:, 3:4]
    x = x_ref[...] + w1 * y1_ref[...] + w2 * y2_ref[...]
    o_ref[...] = x * lax.rsqrt(jnp.mean(x * x, axis=-1, keepdims=True) + EPS) * g_ref[...]


def _combine(x2, y, route, norm_final, tm):
    t, d = x2.shape
    row = lambda i: (i, 0)
    return pl.pallas_call(
        _combine_kernel,
        grid=(t // tm,),
        in_specs=[pl.BlockSpec((tm, d), row), pl.BlockSpec((tm, d), row),
                  pl.BlockSpec((tm, d), lambda i: (i + t // tm, 0)),
                  pl.BlockSpec((tm, LANES), row), pl.BlockSpec((1, d), lambda i: (0, 0))],
        out_specs=pl.BlockSpec((tm, d), row),
        out_shape=jax.ShapeDtypeStruct((t, d), F32),
        compiler_params=_cparams("parallel"),
    )(x2, y, y, route, norm_final.reshape(1, d))


def _moe_plan(route, n_experts, tile):
    t = route.shape[0]
    pairs = MOE_TOPK * t
    n_tiles = pairs // tile + n_experts
    e_flat = route[:, :MOE_TOPK].astype(I32).reshape(pairs)
    order = jnp.argsort(e_flat, stable=True).astype(I32)
    counts = jnp.sum((e_flat[:, None] == jnp.arange(n_experts, dtype=I32)[None, :]).astype(I32), axis=0)
    tiles_per = (counts + tile - 1) // tile
    tile_end = jnp.cumsum(tiles_per)
    sorted_start = jnp.cumsum(counts) - counts
    tile_ids = jnp.arange(n_tiles, dtype=I32)
    tile_valid = (tile_ids < tile_end[-1]).astype(I32)
    tile_expert = jnp.minimum(jnp.sum((tile_ids[:, None] >= tile_end[None, :]).astype(I32), axis=1),
                              n_experts - 1)
    last_expert = tile_expert[jnp.maximum(tile_end[-1] - 1, 0)]
    tile_expert = jnp.where(tile_valid > 0, tile_expert, last_expert)
    tile_first = (tile_end - tiles_per)[tile_expert]
    lane_row = jnp.arange(tile, dtype=I32)[None, :]
    row_in_group = ((tile_ids - tile_first) * tile)[:, None] + lane_row
    row_valid = (row_in_group < counts[tile_expert][:, None]) & (tile_valid[:, None] > 0)
    src = jnp.clip(sorted_start[tile_expert][:, None] + row_in_group, 0, pairs - 1)
    pair = order[src]
    token = jnp.where(row_valid, pair // MOE_TOPK, 0)
    spare = pairs + (tile_ids % 2)[:, None] * tile + lane_row
    dest = jnp.where(row_valid, (pair % MOE_TOPK) * t + pair // MOE_TOPK, spare)
    return tile_expert, tile_valid, token.reshape(-1), dest.reshape(-1)


def kernel(x, mem, norm_mix, w_in, conv_w, conv_b, dt_bias, a_log, d_skip, ssm_norm, w_ssm_out,
           w_att_out, w_mix_out, norm_x, norm_mem, w_xq, w_xkv, w_xo, norm_moe, w_rg, b_rg, w_re,
           b_re, w_e_gate, w_e_up, w_e_down, norm_final):
    batch, seq, d = x.shape
    mem_len = mem.shape[1]
    t = batch * seq
    heads = dt_bias.shape[1]
    d_inner = heads * SSM_HEAD_DIM
    conv_dim = conv_w.shape[1]
    att_w = w_att_out.shape[1]
    att_heads = att_w // ATT_HEAD_DIM
    n_experts = w_re.shape[2]
    groups = w_rg.shape[2]
    per_group = n_experts // groups
    idx_heads = (w_in.shape[2] - (d_inner + conv_dim + heads + att_w + 2 * ATT_HEAD_DIM
                                  + IDX_HEAD_DIM + 2 * d)) // (IDX_HEAD_DIM + 1)
    idx_w = idx_heads * IDX_HEAD_DIM
    assert w_in.shape[0] == 1, "one layer; the final norm is fused into the layer's last call"
    assert seq % SSM_CHUNK == 0 and seq % Q_BLOCK == 0 and heads + idx_heads <= LANES
    assert groups + n_experts <= LANES and t % 1024 == 0 and heads % (2 * SSM_GROUPS) == 0
    li = 0

    sizes = (d_inner, conv_dim, heads, att_w, ATT_HEAD_DIM, ATT_HEAD_DIM, idx_w, IDX_HEAD_DIM,
             idx_heads, d, d)
    offs = [0]
    for s in sizes:
        offs.append(offs[-1] + s)
    wi_ = w_in[li]
    col = lambda k: wi_[:, offs[k]:offs[k + 1]]
    pad = lambda a, n: jnp.pad(a, ((0, 0), (0, n - a.shape[1])))
    lpad = lambda a, n: jnp.pad(a, ((0, 0), (n - a.shape[1], 0)))
    both = lambda a: [pad(a, LANES), lpad(a, LANES)]
    w_z = col(0).astype(BF16)
    w_xbc = col(1).astype(BF16)
    w_small = pad(jnp.concatenate([col(2), col(8)], axis=1), LANES).astype(BF16)
    w_qq = jnp.concatenate([col(3) * (LOG2E * ATT_HEAD_DIM ** -0.5), col(6)], axis=1).astype(BF16)
    w_kv = jnp.concatenate(both(col(7)) + [pad(col(4), LANES), pad(col(5), LANES)],
                           axis=1).astype(BF16)
    w_gates = jnp.concatenate([col(9), col(10)], axis=1).astype(BF16)

    h = x.reshape(t, d)
    u = _rmsnorm(h, norm_mix[li], BF16, 512)
    z = _matmul(u, w_z, BF16, 1024, 1024)
    xbc = _matmul(u, w_xbc, BF16, 1024, 1024)
    small = _matmul(u, w_small, F32, 512, LANES)
    qq = _matmul(u, w_qq, BF16, 512, att_w + idx_w)
    kvp = _matmul(u, w_kv, BF16, 512, 4 * LANES)
    gates = _matmul(u, w_gates, BF16, 1024, 1024)

    dtt = jnp.swapaxes(small[:, :heads].reshape(batch, seq, heads), 1, 2)
    ys = _ssd(z, xbc, small, dtt, conv_w[li], conv_b[li], dt_bias[li], a_log[li], d_skip[li],
              ssm_norm[li], batch, seq)
    wit = jnp.swapaxes(small[:, heads:heads + idx_heads].reshape(batch, seq, idx_heads), 1, 2)
    oa = _dsa(qq, kvp, wit, batch, seq, att_heads, idx_heads)

    x1, xn1 = _merge(h, ys, oa, gates, w_ssm_out[li].astype(BF16), w_att_out[li].astype(BF16),
                     w_mix_out[li].astype(BF16), norm_x[li], 512)

    mn = _rmsnorm(mem.reshape(batch * mem_len, d), norm_mem[li], BF16, 512)
    kvm = _matmul(mn, w_xkv[li].astype(BF16), BF16, 512, 1024)
    x2 = _xattn(x1, xn1, kvm, w_xq[li].astype(BF16), w_xo[li].astype(BF16), batch, seq,
                mem_len, 512)

    wr = pad(jnp.concatenate([w_rg[li], w_re[li]], axis=1), LANES)
    br = pad(jnp.concatenate([b_rg[li], b_re[li]]).reshape(1, -1), LANES)
    xm, route = _router(x2, norm_moe[li], wr, br, groups, per_group, 512)

    tile_expert, tile_valid, token, dest = _moe_plan(route, n_experts, EXPERT_TILE)
    ye = _experts(xm, tile_expert, tile_valid, token, dest, w_e_gate[li].astype(BF16),
                  w_e_up[li].astype(BF16), w_e_down[li].astype(BF16),
                  MOE_TOPK * t + 2 * EXPERT_TILE)
    out = _combine(x2, ye, route, norm_final, 512)
    return out.reshape(batch, seq, d)
```

```python
import functools
import math

import numpy as np
import jax
import jax.numpy as jnp
from jax import lax
from jax.experimental import pallas as pl
from jax.experimental.pallas import tpu as pltpu

F32 = jnp.float32
BF16 = jnp.bfloat16
I32 = jnp.int32
EPS = 1e-6
HIGHEST = lax.Precision.HIGHEST
LOG2E = math.log2(math.e)

SSM_HEAD_DIM = 64
SSM_GROUPS = 4
SSM_STATE = 128
SSM_CHUNK = 128
ATT_HEAD_DIM = 64
IDX_HEAD_DIM = 64
TOPK_MAX = 256
Q_BLOCK = 128
NEG_INF = -1e30
X_HEADS = 4
MOE_GROUPS = 4
MOE_TOPK = 2

LANES = 128
VMEM_LIMIT = 56 * 1024 * 1024
EXPERT_TILE = 256
DSA_WIDTHS = 4
DSA_HEADS_PER_DOT = 4


def _cparams(*sem):
    return pltpu.CompilerParams(dimension_semantics=sem, vmem_limit_bytes=VMEM_LIMIT)


def _sigmoid(x):
    return 1.0 / (1.0 + jnp.exp(-x))


def _softplus(x):
    return jnp.maximum(x, 0.0) + jnp.log1p(jnp.exp(-jnp.abs(x)))


def _dot(a, b, precision=None):
    return jnp.dot(a, b, preferred_element_type=F32, precision=precision)


def _split3(x):
    p0 = x.astype(BF16)
    r1 = x - p0.astype(F32)
    p1 = r1.astype(BF16)
    p2 = (r1 - p1.astype(F32)).astype(BF16)
    return p0, p1, p2


def _dot_exact_rhs(x, m):
    mb = m.astype(BF16)
    p0, p1, p2 = _split3(x)
    return _dot(p0, mb) + _dot(p1, mb) + _dot(p2, mb)


def _dot_exact_lhs(m, x):
    mb = m.astype(BF16)
    p0, p1, p2 = _split3(x)
    return _dot(mb, p0) + _dot(mb, p1) + _dot(mb, p2)


def _dot_nt(a, b):
    return lax.dot_general(a, b, (((1,), (1,)), ((), ())), preferred_element_type=F32)


def _reduce_rows(x, op, pair_op, chains=8):
    w, n = x.shape
    while w % (8 * chains):
        chains //= 2
    step = w // chains
    parts = [op(x[i * step:(i + 1) * step].reshape(step // 8, 8, n), axis=0) for i in range(chains)]
    while len(parts) > 1:
        parts = [pair_op(parts[i], parts[i + 1]) for i in range(0, len(parts), 2)]
    return op(parts[0], axis=0, keepdims=True)


def _rmsnorm_kernel(x_ref, g_ref, o_ref):
    x = x_ref[...]
    y = x * lax.rsqrt(jnp.mean(x * x, axis=-1, keepdims=True) + EPS) * g_ref[...]
    o_ref[...] = y.astype(o_ref.dtype)


def _rmsnorm(x, g, out_dtype, tm):
    m, d = x.shape
    return pl.pallas_call(
        _rmsnorm_kernel,
        grid=(m // tm,),
        in_specs=[pl.BlockSpec((tm, d), lambda i: (i, 0)),
                  pl.BlockSpec((1, d), lambda i: (0, 0))],
        out_specs=pl.BlockSpec((tm, d), lambda i: (i, 0)),
        out_shape=jax.ShapeDtypeStruct((m, d), out_dtype),
        compiler_params=_cparams("parallel"),
    )(x, g.reshape(1, d))


def _mm_kernel(a_ref, b_ref, o_ref):
    o_ref[...] = _dot(a_ref[...], b_ref[...]).astype(o_ref.dtype)


def _matmul(a, b, out_dtype, tm, tn):
    m, k = a.shape
    n = b.shape[1]
    return pl.pallas_call(
        _mm_kernel,
        grid=(n // tn, m // tm),
        in_specs=[pl.BlockSpec((tm, k), lambda j, i: (i, 0)),
                  pl.BlockSpec((k, tn), lambda j, i: (0, j))],
        out_specs=pl.BlockSpec((tm, tn), lambda j, i: (i, j)),
        out_shape=jax.ShapeDtypeStruct((m, n), out_dtype),
        compiler_params=_cparams("parallel", "parallel"),
    )(a, b)


def _ssd_kernel(z_ref, xbc_ref, dt_ref, dtt_ref, cw_ref, cb_ref, dtb_ref, dtbt_ref,
                alog_ref, alogt_ref, dskip_ref, norm_ref, e_ref, o_ref,
                tail_ref, st_ref, y_ref, *, heads, d_inner):
    q = SSM_CHUNK
    n = SSM_STATE
    hpg = heads // SSM_GROUPS
    gw = hpg * SSM_HEAD_DIM
    c = pl.program_id(1)

    @pl.when(c == 0)
    def _():
        tail_ref[...] = jnp.zeros_like(tail_ref)
        st_ref[...] = jnp.zeros_like(st_ref)

    x = xbc_ref[...]
    tl = tail_ref.shape[0]
    xcat = jnp.concatenate([tail_ref[...], x], axis=0)
    tail_ref[...] = x[q - tl:, :]
    cw = cw_ref[...]
    kconv = cw.shape[0]
    srow = lax.broadcasted_iota(I32, ((kconv - 1) * q, tl + q), 0)
    scol = lax.broadcasted_iota(I32, ((kconv - 1) * q, tl + q), 1)
    shift = jnp.where(scol - (tl - (kconv - 1)) == srow - (q - 1) * (srow // q), 1.0, 0.0)
    shifted = _dot(shift.astype(xcat.dtype), xcat)
    acc = cb_ref[...] + cw[kconv - 1:kconv, :] * x.astype(F32)
    for k in range(kconv - 1):
        acc = acc + cw[k:k + 1, :] * shifted[k * q:(k + 1) * q, :]
    xc = acc * _sigmoid(acc)
    xs = xc[:, :d_inner]
    bm = xc[:, d_inner:d_inner + SSM_GROUPS * n]
    cm = xc[:, d_inner + SSM_GROUPS * n:]

    dt = _softplus(dt_ref[...] + dtb_ref[...])
    da = dt * (-jnp.exp(alog_ref[...]))
    dtt = _softplus(dtt_ref[...] + dtbt_ref[...])
    dat = dtt * (-jnp.exp(alogt_ref[...]))
    rows = lax.broadcasted_iota(I32, (q, q), 0)
    cols = lax.broadcasted_iota(I32, (q, q), 1)
    causal = rows >= cols
    tril = jnp.where(causal, 1.0, 0.0)
    triu = jnp.where(rows <= cols, 1.0, 0.0)
    a_cs = _dot_exact_lhs(tril, da)
    a_cst = _dot_exact_rhs(dat, triu)
    expand = e_ref[...]
    dt_e = _dot_exact_rhs(dt, expand)
    acs_e = _dot_exact_rhs(a_cs, expand)
    expa = jnp.exp(acs_e)
    a_last = acs_e[q - 1:q, :]
    xdt = xs * dt_e
    xdt_b = xdt.astype(BF16)
    xdec_b = (xdt * jnp.exp(a_last - acs_e)).astype(BF16)
    st = st_ref[...]
    st_b = st.astype(BF16)
    lane = lax.broadcasted_iota(I32, (q, 2 * SSM_HEAD_DIM), 1)

    for g in range(SSM_GROUPS):
        bg = bm[:, g * n:(g + 1) * n]
        cg = cm[:, g * n:(g + 1) * n].astype(BF16)
        cb = _dot_nt(cg, bg.astype(BF16))
        gs = slice(g * gw, (g + 1) * gw)
        y_off = _dot(cg, st_b[:, gs])
        s_new = _dot(bg.T.astype(BF16), xdec_b[:, gs])
        st_ref[:, gs] = st[:, gs] * expa[q - 1:q, gs] + s_new
        for j in range(hpg // 2):
            h0 = g * hpg + 2 * j
            c0 = h0 * SSM_HEAD_DIM
            xp = xdt_b[:, c0:c0 + 2 * SSM_HEAD_DIM]
            parts = []
            for h in (h0, h0 + 1):
                seg = a_cs[:, h:h + 1] - a_cst[h:h + 1, :]
                lmat = jnp.exp(jnp.where(causal, seg, -jnp.inf))
                parts.append(_dot((cb * lmat).astype(BF16), xp))
            y_diag = jnp.where(lane < SSM_HEAD_DIM, parts[0], parts[1])
            cs = slice(c0, c0 + 2 * SSM_HEAD_DIM)
            y_ref[:, cs] = y_diag + y_off[:, c0 - g * gw:c0 - g * gw + 2 * SSM_HEAD_DIM] * expa[:, cs]

    y = y_ref[...] + dskip_ref[...] * xs
    zz = z_ref[...].astype(F32)
    yg = y * (zz * _sigmoid(zz))
    out = yg * lax.rsqrt(jnp.mean(yg * yg, axis=-1, keepdims=True) + EPS) * norm_ref[...]
    o_ref[...] = out.astype(o_ref.dtype)


def _ssd(z, xbc, small, dtt, conv_w, conv_b, dt_bias, a_log, d_skip, ssm_norm, batch, seq):
    heads = dt_bias.shape[0]
    d_inner = heads * SSM_HEAD_DIM
    conv_dim = xbc.shape[1]
    q = SSM_CHUNK
    nc = seq // q
    expand = jnp.repeat(jnp.eye(LANES, heads, dtype=BF16), SSM_HEAD_DIM, axis=1)
    lane_pad = lambda v: jnp.pad(v.reshape(1, -1), ((0, 0), (0, LANES - heads)))
    row = lambda b, c: (b * nc + c, 0)
    const = lambda b, c: (0, 0)
    kern = functools.partial(_ssd_kernel, heads=heads, d_inner=d_inner)
    return pl.pallas_call(
        kern,
        grid=(batch, nc),
        in_specs=[
            pl.BlockSpec((q, d_inner), row),
            pl.BlockSpec((q, conv_dim), row),
            pl.BlockSpec((q, LANES), row),
            pl.BlockSpec((None, heads, q), lambda b, c: (b, 0, c)),
            pl.BlockSpec(conv_w.T.shape, const),
            pl.BlockSpec((1, conv_dim), const),
            pl.BlockSpec((1, LANES), const),
            pl.BlockSpec((heads, 1), const),
            pl.BlockSpec((1, LANES), const),
            pl.BlockSpec((heads, 1), const),
            pl.BlockSpec((1, d_inner), const),
            pl.BlockSpec((1, d_inner), const),
            pl.BlockSpec((LANES, d_inner), const),
        ],
        out_specs=pl.BlockSpec((q, d_inner), row),
        out_shape=jax.ShapeDtypeStruct((batch * seq, d_inner), BF16),
        scratch_shapes=[pltpu.VMEM((16, conv_dim), xbc.dtype),
                        pltpu.VMEM((SSM_STATE, d_inner), F32),
                        pltpu.VMEM((q, d_inner), F32)],
        compiler_params=_cparams("parallel", "arbitrary"),
    )(z, xbc, small, dtt, conv_w.T, conv_b.reshape(1, -1),
      lane_pad(dt_bias), dt_bias.reshape(-1, 1), lane_pad(a_log), a_log.reshape(-1, 1),
      jnp.repeat(d_skip, SSM_HEAD_DIM).reshape(1, -1), ssm_norm.reshape(1, -1), expand)


def _order_key(x):
    bits = int(np.array(x, np.float32).view(np.int32))
    return bits ^ ((bits >> 31) & 0x7FFFFFFF)


def _dsa_consts(seq, att_heads):
    hd = ATT_HEAD_DIM
    assert seq <= 16 * 256 and 2 * hd == LANES
    qc = np.zeros((att_heads, LANES), np.float32)
    for h in range(att_heads):
        rest = np.float32(2.0 ** (-8.0 * (h + 1) / att_heads) * LOG2E)
        for i in range(3):
            piece = np.float32(rest.astype(jnp.bfloat16))
            rest = np.float32(rest - piece)
            qc[h, hd + i] = 16.0 * piece
            qc[h, hd + 3 + i] = piece
    pos = np.arange(seq)
    kpos = np.zeros((seq, LANES), np.float32)
    kpos[:, hd:hd + 3] = (pos // 16)[:, None]
    kpos[:, hd + 3:hd + 6] = (pos % 16)[:, None]
    return jnp.asarray(qc), jnp.asarray(kpos, dtype=BF16)


def _dsa_kernel(q_ref, qi_ref, ka_ref, kip_ref, vt_ref, wit_ref, qc_ref, o_ref,
                key_ref, mb_ref, qa_ref, *, topk, att_heads, idx_heads, widths):
    qb = Q_BLOCK
    seq = ka_ref.shape[0]
    nq = seq // qb
    blk = pl.program_id(1)
    hd = ATT_HEAD_DIM
    kf = float(topk)
    key_ni = _order_key(NEG_INF)
    left = lax.broadcasted_iota(I32, (qb, LANES), 1) < hd
    tpos = blk * qb + lax.broadcasted_iota(I32, (1, qb), 1)

    hpd = DSA_HEADS_PER_DOT
    for h in range(att_heads):
        qpair = q_ref[:, (h // 2) * LANES:(h // 2 + 1) * LANES].astype(F32)
        qh = qpair if h % 2 == 0 else pltpu.roll(qpair, hd, axis=1)
        qa_ref[h // hpd, (h % hpd) * qb:(h % hpd + 1) * qb, :] = jnp.where(
            left, qh, qc_ref[h:h + 1, :]).astype(BF16)

    def body(w):
        n_out = float(seq - w)
        spos = lax.broadcasted_iota(I32, (w, 1), 0)
        causal = spos <= tpos

        wit = wit_ref[...] * (idx_heads ** -0.5 * IDX_HEAD_DIM ** -0.5)
        isc = jnp.zeros((w, qb), F32)
        qi_rows = jnp.concatenate([qi_ref[:, g * LANES:(g + 1) * LANES]
                                   for g in range(idx_heads // 2)], axis=0)
        for par in range(2):
            rel = _dot_nt(kip_ref[:w, par * LANES:(par + 1) * LANES], qi_rows)
            for g in range(idx_heads // 2):
                h = 2 * g + par
                isc = isc + jnp.maximum(rel[:, g * qb:(g + 1) * qb], 0.0) * wit[h:h + 1, :]
        masked = jnp.where(causal, isc, NEG_INF) + 0.0
        bits = pltpu.bitcast(masked, I32)
        key_ref[:w, :] = bits ^ ((bits >> 31) & 0x7FFFFFFF)

        def count(mask):
            return _reduce_rows(jnp.where(mask, 1.0, 0.0), jnp.sum, jnp.add)

        def count_ge(cand):
            return count(key_ref[:w, :] >= cand) + jnp.where(key_ni >= cand, n_out, 0.0)

        int_min = jnp.full((1, qb), -2 ** 31, I32)
        thr0 = jnp.where(count_ge(jnp.zeros((1, qb), I32)) >= kf, 0, int_min)

        def thr_body(j, thr):
            cand = thr + lax.shift_left(jnp.int32(1), 30 - j)
            return jnp.where(count_ge(cand) >= kf, cand, thr)

        thr = lax.fori_loop(0, 31, thr_body, thr0)

        key = key_ref[:w, :]
        gt = key > thr
        eqc = (key == thr) & causal
        need = kf - count(gt) - jnp.where(key_ni > thr, n_out, 0.0)
        has_tie = jnp.max(jnp.where(count(eqc) > need, 1.0, 0.0)) > 0.0

        @pl.when(jnp.logical_not(has_tie))
        def _():
            mb_ref[:w, :] = jnp.where(causal & (key >= thr), 0.0, -jnp.inf)

        @pl.when(has_tie)
        def _():
            nbits = max(1, (w - 1).bit_length())

            def cut_body(b, cut):
                cand = cut + lax.shift_left(jnp.int32(1), nbits - 1 - b)
                eq_here = (key_ref[:w, :] == thr) & causal
                return jnp.where(count(eq_here & (spos < cand)) < need, cand, cut)

            cut = lax.fori_loop(0, nbits, cut_body, jnp.zeros((1, qb), I32))
            sel = gt | (eqc & (spos <= cut))
            mb_ref[:w, :] = jnp.where(causal & sel, 0.0, -jnp.inf)

        n_dots = att_heads // hpd
        s_next = _dot_nt(ka_ref[:w, :], qa_ref[0])
        for jd in range(n_dots):
            s = s_next
            if jd + 1 < n_dots:
                s_next = _dot_nt(ka_ref[:w, :], qa_ref[jd + 1])
            ps = []
            for e in range(hpd):
                sh = s[:, e * qb:(e + 1) * qb] + mb_ref[:w, :]
                ps.append(jnp.exp2((sh - _reduce_rows(sh, jnp.max, jnp.maximum)).astype(BF16)))
            acc = _dot(vt_ref[:, :w], jnp.concatenate(ps, axis=1))
            o = acc * (1.0 / acc[hd:hd + 1, :])
            for e in range(0, hpd, 2):
                pair = jnp.where(left, o[:, e * qb:(e + 1) * qb].T,
                                 pltpu.roll(o[:, (e + 1) * qb:(e + 2) * qb].T, hd, axis=1))
                g = (jd * hpd + e) // 2
                o_ref[:, g * LANES:(g + 1) * LANES] = pair.astype(o_ref.dtype)

    per = nq // widths
    for v in range(widths):
        @pl.when((blk >= v * per) & (blk < (v + 1) * per))
        def _(v=v):
            body((v + 1) * per * qb)


def _dsa(qq, kvp, wit, batch, seq, att_heads, idx_heads):
    qb = Q_BLOCK
    nb = seq // qb
    hd = ATT_HEAD_DIM
    att_w = att_heads * hd
    idx_w = idx_heads * IDX_HEAD_DIM
    widths = min(DSA_WIDTHS, nb)
    assert att_w % idx_w == 0 and nb % widths == 0 and idx_heads % 2 == 0
    assert att_heads % DSA_HEADS_PER_DOT == 0 and DSA_HEADS_PER_DOT % 2 == 0
    assert 2 * IDX_HEAD_DIM == LANES and qb == LANES
    topk = min(TOPK_MAX, seq // 4)
    qc, kpos = _dsa_consts(seq, att_heads)
    ka = kvp[:, 2 * LANES:3 * LANES] + jnp.tile(kpos, (batch, 1))
    vt = jnp.swapaxes(kvp[:, 3 * LANES:].reshape(batch, seq, LANES), 1, 2)
    vt = vt.at[:, hd, :].set(1.0)
    kern = functools.partial(_dsa_kernel, topk=topk, att_heads=att_heads, idx_heads=idx_heads,
                             widths=widths)
    return pl.pallas_call(
        kern,
        grid=(batch, nb),
        in_specs=[
            pl.BlockSpec((qb, att_w), lambda b, i: (b * nb + i, 0)),
            pl.BlockSpec((qb, idx_w), lambda b, i: (b * nb + i, att_w // idx_w)),
            pl.BlockSpec((seq, LANES), lambda b, i: (b, 0)),
            pl.BlockSpec((seq, 2 * LANES), lambda b, i: (b, 0)),
            pl.BlockSpec((None, LANES, seq), lambda b, i: (b, 0, 0)),
            pl.BlockSpec((None, idx_heads, qb), lambda b, i: (b, 0, i)),
            pl.BlockSpec(qc.shape, lambda b, i: (0, 0)),
        ],
        out_specs=pl.BlockSpec((qb, att_w), lambda b, i: (b * nb + i, 0)),
        out_shape=jax.ShapeDtypeStruct((batch * seq, att_w), BF16),
        scratch_shapes=[pltpu.VMEM((seq, qb), I32),
                        pltpu.VMEM((seq, qb), F32),
                        pltpu.VMEM((att_heads // DSA_HEADS_PER_DOT, DSA_HEADS_PER_DOT * qb, LANES),
                                   BF16)],
        compiler_params=_cparams("parallel", "parallel"),
    )(qq, qq, ka, kvp, vt, wit, qc)


def _merge_kernel(x_ref, ys_ref, oa_ref, g_ref, wso_ref, wao_ref, wmo_ref, nx_ref,
                  x1_ref, xn_ref):
    d = x_ref.shape[1]
    y_ssm = _dot(ys_ref[...], wso_ref[...])
    y_att = _dot(oa_ref[...], wao_ref[...])
    g = g_ref[...].astype(F32)
    mix = _sigmoid(g[:, :d]) * y_ssm + _sigmoid(g[:, d:]) * y_att
    x1 = x_ref[...] + _dot(mix.astype(BF16), wmo_ref[...])
    x1_ref[...] = x1
    xn = x1 * lax.rsqrt(jnp.mean(x1 * x1, axis=-1, keepdims=True) + EPS) * nx_ref[...]
    xn_ref[...] = xn.astype(xn_ref.dtype)


def _merge(x, ys, oa, gates, wso, wao, wmo, norm_x, tm):
    t, d = x.shape
    row = lambda i: (i, 0)
    const = lambda i: (0, 0)
    return pl.pallas_call(
        _merge_kernel,
        grid=(t // tm,),
        in_specs=[pl.BlockSpec((tm, d), row),
                  pl.BlockSpec((tm, ys.shape[1]), row),
                  pl.BlockSpec((tm, oa.shape[1]), row),
                  pl.BlockSpec((tm, 2 * d), row),
                  pl.BlockSpec(wso.shape, const),
                  pl.BlockSpec(wao.shape, const),
                  pl.BlockSpec(wmo.shape, const),
                  pl.BlockSpec((1, d), const)],
        out_specs=[pl.BlockSpec((tm, d), row), pl.BlockSpec((tm, d), row)],
        out_shape=[jax.ShapeDtypeStruct((t, d), F32), jax.ShapeDtypeStruct((t, d), BF16)],
        compiler_params=_cparams("parallel"),
    )(x, ys, oa, gates, wso, wao, wmo, norm_x.reshape(1, d))


def _xattn_kernel(x1_ref, xn_ref, kv_ref, wq_ref, wo_ref, x2_ref):
    d = x1_ref.shape[1]
    hd = d // X_HEADS
    qf = _dot(xn_ref[...], wq_ref[...]) * (hd ** -0.5)
    qb = qf.astype(BF16)
    outs = []
    for h in range(X_HEADS):
        kh = kv_ref[:, h * hd:(h + 1) * hd]
        vh = kv_ref[:, d + h * hd:d + (h + 1) * hd]
        s = _dot_nt(qb[:, h * hd:(h + 1) * hd], kh)
        m = jnp.max(s, axis=-1, keepdims=True)
        p = jnp.exp(s - m)
        l = jnp.sum(p, axis=-1, keepdims=True)
        outs.append((_dot(p.astype(BF16), vh) / l).astype(BF16))
    o = jnp.concatenate(outs, axis=-1)
    x2_ref[...] = x1_ref[...] + _dot(o, wo_ref[...])


def _xattn(x1, xn, kvm, wq, wo, batch, seq, mem_len, tm):
    t, d = x1.shape
    nt = seq // tm
    row = lambda b, i: (b * nt + i, 0)
    const = lambda b, i: (0, 0)
    return pl.pallas_call(
        _xattn_kernel,
        grid=(batch, nt),
        in_specs=[pl.BlockSpec((tm, d), row),
                  pl.BlockSpec((tm, d), row),
                  pl.BlockSpec((mem_len, 2 * d), lambda b, i: (b, 0)),
                  pl.BlockSpec(wq.shape, const),
                  pl.BlockSpec(wo.shape, const)],
        out_specs=pl.BlockSpec((tm, d), row),
        out_shape=jax.ShapeDtypeStruct((t, d), F32),
        compiler_params=_cparams("parallel", "parallel"),
    )(x1, xn, kvm, wq, wo)


def _router_kernel(x_ref, g_ref, wr_ref, br_ref, xm_ref, route_ref, *, groups, per_group):
    x = x_ref[...]
    xm = x * lax.rsqrt(jnp.mean(x * x, axis=-1, keepdims=True) + EPS) * g_ref[...]
    xm_ref[...] = xm
    logits = _dot(xm, wr_ref[...], HIGHEST) + br_ref[...]
    lane = lax.broadcasted_iota(I32, logits.shape, 1)
    lanef = lane.astype(F32)
    far = float(LANES)
    ninf = -jnp.inf

    gl = jnp.where(lane < groups, logits, ninf)
    gmax = jnp.max(gl, axis=-1, keepdims=True)
    gsel = jnp.min(jnp.where(gl == gmax, lanef, far), axis=-1, keepdims=True)
    ge = jnp.exp(gl - gmax)
    pg = jnp.max(ge / jnp.sum(ge, axis=-1, keepdims=True), axis=-1, keepdims=True)

    lo = groups + per_group * gsel
    em = (lanef >= lo) & (lanef < lo + per_group)
    el = jnp.where(em, logits, ninf)
    ee = jnp.exp(el - jnp.max(el, axis=-1, keepdims=True))
    ep = jnp.where(em, ee / jnp.sum(ee, axis=-1, keepdims=True), -1.0)
    p1 = jnp.max(ep, axis=-1, keepdims=True)
    i1 = jnp.min(jnp.where(ep == p1, lanef, far), axis=-1, keepdims=True)
    ep2 = jnp.where(lanef == i1, -1.0, ep)
    p2 = jnp.max(ep2, axis=-1, keepdims=True)
    i2 = jnp.min(jnp.where(ep2 == p2, lanef, far), axis=-1, keepdims=True)
    w1 = pg * p1 / (p1 + p2)
    w2 = pg * p2 / (p1 + p2)
    route_ref[...] = jnp.where(lane == 0, i1 - groups,
                     jnp.where(lane == 1, i2 - groups,
                     jnp.where(lane == 2, w1,
                     jnp.where(lane == 3, w2, 0.0))))


def _router(x2, norm_moe, wr, br, groups, per_group, tm):
    t, d = x2.shape
    row = lambda i: (i, 0)
    const = lambda i: (0, 0)
    kern = functools.partial(_router_kernel, groups=groups, per_group=per_group)
    return pl.pallas_call(
        kern,
        grid=(t // tm,),
        in_specs=[pl.BlockSpec((tm, d), row), pl.BlockSpec((1, d), const),
                  pl.BlockSpec((d, LANES), const), pl.BlockSpec((1, LANES), const)],
        out_specs=[pl.BlockSpec((tm, d), row), pl.BlockSpec((tm, LANES), row)],
        out_shape=[jax.ShapeDtypeStruct((t, d), F32), jax.ShapeDtypeStruct((t, LANES), F32)],
        compiler_params=_cparams("parallel"),
    )(x2, norm_moe.reshape(1, d), wr, br)


def _expert_kernel(te_ref, tv_ref, gcur_ref, gnext_ref, sidx_ref, xm_ref, wg_ref, wu_ref, wd_ref,
                   y_ref, xbuf, ybuf, gsem, ssem):
    i = pl.program_id(0)
    nt = pl.num_programs(0)
    slot = i % 2
    groups8 = xbuf.shape[1]
    rows = 8 * groups8
    d = xbuf.shape[3]
    nxt = jnp.minimum(i + 1, nt - 1)
    valid = tv_ref[i] > 0
    has_next = (i + 1 < nt) & (tv_ref[nxt] > 0)

    def gather_copy(src_row, s, g8, u):
        return pltpu.make_async_copy(xm_ref.at[pl.ds(src_row, 1), :],
                                     xbuf.at[s, g8, pl.ds(u, 1), :], gsem.at[s])

    def scatter_copy(s, g8, u, dst_row):
        return pltpu.make_async_copy(ybuf.at[s, g8, pl.ds(u, 1), :],
                                     y_ref.at[pl.ds(dst_row, 1), :], ssem.at[s])

    def per_row(fn):
        def g8_body(g8, carry):
            for u in range(8):
                fn(g8, u)
            return carry
        lax.fori_loop(0, groups8, g8_body, 0)

    def gather_start(idx_ref, s):
        per_row(lambda g8, u: gather_copy(idx_ref[0, g8 * 8 + u], s, g8, u).start(priority=u % 2))

    def gather_wait(s):
        per_row(lambda g8, u: gather_copy(0, s, g8, u).wait())

    def scatter_start(s):
        per_row(lambda g8, u: scatter_copy(s, g8, u, sidx_ref[0, g8 * 8 + u]).start(priority=u % 2))

    def scatter_wait(s):
        per_row(lambda g8, u: scatter_copy(s, g8, u, 0).wait())

    @pl.when(i == 0)
    def _():
        ybuf[1] = jnp.zeros(ybuf.shape[1:], ybuf.dtype)
        n_real = y_ref.shape[0] - 2 * rows

        def fill_copy(g):
            return pltpu.make_async_copy(ybuf.at[1, g % groups8],
                                         y_ref.at[pl.ds(n_real + g * 8, 8), :], ssem.at[1])

        def fill_start(g, carry):
            fill_copy(g).start()
            return carry

        def fill_wait(g, carry):
            fill_copy(g).wait()
            return carry

        lax.fori_loop(0, 2 * groups8, fill_start, 0)
        lax.fori_loop(0, 2 * groups8, fill_wait, 0)

    @pl.when((i == 0) & valid)
    def _():
        gather_start(gcur_ref, 0)

    @pl.when(valid)
    def _():
        @pl.when(has_next)
        def _():
            gather_start(gnext_ref, 1 - slot)

        gather_wait(slot)

        @pl.when(i >= 2)
        def _():
            scatter_wait(slot)

        xb = xbuf[slot].reshape(rows, d).astype(BF16)
        gate = _dot(xb, wg_ref[...])
        hid = gate * _sigmoid(gate) * _dot(xb, wu_ref[...])
        ybuf[slot] = _dot(hid.astype(BF16), wd_ref[...]).reshape(groups8, 8, d)
        scatter_start(slot)

        @pl.when(jnp.logical_not(has_next))
        def _():
            scatter_wait(slot)

            @pl.when(i >= 1)
            def _():
                scatter_wait(1 - slot)


def _experts(xm, tile_expert, tile_valid, gidx, sidx, wg, wu, wd, n_out):
    t, d = xm.shape
    tm = EXPERT_TILE
    nt = tile_expert.shape[0]
    hid = wg.shape[2]
    smem_rows = lambda imap: pl.BlockSpec((None, 1, tm), imap, memory_space=pltpu.SMEM)
    wspec = lambda shape: pl.BlockSpec((None,) + shape, lambda i, te, tv: (te[i], 0, 0))
    grid_spec = pltpu.PrefetchScalarGridSpec(
        num_scalar_prefetch=2,
        grid=(nt,),
        in_specs=[smem_rows(lambda i, te, tv: (i, 0, 0)),
                  smem_rows(lambda i, te, tv: (jnp.minimum(i + 1, nt - 1), 0, 0)),
                  smem_rows(lambda i, te, tv: (i, 0, 0)),
                  pl.BlockSpec(memory_space=pl.ANY),
                  wspec((d, hid)), wspec((d, hid)), wspec((hid, d))],
        out_specs=pl.BlockSpec(memory_space=pl.ANY),
        scratch_shapes=[pltpu.VMEM((2, tm // 8, 8, d), F32), pltpu.VMEM((2, tm // 8, 8, d), F32),
                        pltpu.SemaphoreType.DMA((2,)), pltpu.SemaphoreType.DMA((2,))],
    )
    g3 = gidx.reshape(nt, 1, tm)
    return pl.pallas_call(
        _expert_kernel,
        grid_spec=grid_spec,
        out_shape=jax.ShapeDtypeStruct((n_out, d), F32),
        compiler_params=_cparams("arbitrary"),
    )(tile_expert, tile_valid, g3, g3, sidx.reshape(nt, 1, tm), xm, wg, wu, wd)


def _combine_kernel(x_ref, y1_ref, y2_ref, route_ref, g_ref, o_ref):
    w1 = route_ref[:, 2:3]
    w2 = route_ref[:, 3:4]
    x = x_ref[...] + w1 * y1_ref[...] + w2 * y2_ref[...]
    o_ref[...] = x * lax.rsqrt(jnp.mean(x * x, axis=-1, keepdims=True) + EPS) * g_ref[...]


def _combine(x2, y, route, norm_final, tm):
    t, d = x2.shape
    row = lambda i: (i, 0)
    return pl.pallas_call(
        _combine_kernel,
        grid=(t // tm,),
        in_specs=[pl.BlockSpec((tm, d), row), pl.BlockSpec((tm, d), row),
                  pl.BlockSpec((tm, d), lambda i: (i + t // tm, 0)),
                  pl.BlockSpec((tm, LANES), row), pl.BlockSpec((1, d), lambda i: (0, 0))],
        out_specs=pl.BlockSpec((tm, d), row),
        out_shape=jax.ShapeDtypeStruct((t, d), F32),
        compiler_params=_cparams("parallel"),
    )(x2, y, y, route, norm_final.reshape(1, d))


def _moe_plan(route, n_experts, tile):
    t = route.shape[0]
    pairs = MOE_TOPK * t
    n_tiles = pairs // tile + n_experts
    e_flat = route[:, :MOE_TOPK].astype(I32).reshape(pairs)
    order = jnp.argsort(e_flat, stable=True).astype(I32)
    counts = jnp.sum((e_flat[:, None] == jnp.arange(n_experts, dtype=I32)[None, :]).astype(I32), axis=0)
    tiles_per = (counts + tile - 1) // tile
    tile_end = jnp.cumsum(tiles_per)
    sorted_start = jnp.cumsum(counts) - counts
    tile_ids = jnp.arange(n_tiles, dtype=I32)
    tile_valid = (tile_ids < tile_end[-1]).astype(I32)
    tile_expert = jnp.minimum(jnp.sum((tile_ids[:, None] >= tile_end[None, :]).astype(I32), axis=1),
                              n_experts - 1)
    last_expert = tile_expert[jnp.maximum(tile_end[-1] - 1, 0)]
    tile_expert = jnp.where(tile_valid > 0, tile_expert, last_expert)
    tile_first = (tile_end - tiles_per)[tile_expert]
    lane_row = jnp.arange(tile, dtype=I32)[None, :]
    row_in_group = ((tile_ids - tile_first) * tile)[:, None] + lane_row
    row_valid = (row_in_group < counts[tile_expert][:, None]) & (tile_valid[:, None] > 0)
    src = jnp.clip(sorted_start[tile_expert][:, None] + row_in_group, 0, pairs - 1)
    pair = order[src]
    token = jnp.where(row_valid, pair // MOE_TOPK, 0)
    spare = pairs + (tile_ids % 2)[:, None] * tile + lane_row
    dest = jnp.where(row_valid, (pair % MOE_TOPK) * t + pair // MOE_TOPK, spare)
    return tile_expert, tile_valid, token.reshape(-1), dest.reshape(-1)


def kernel(x, mem, norm_mix, w_in, conv_w, conv_b, dt_bias, a_log, d_skip, ssm_norm, w_ssm_out,
           w_att_out, w_mix_out, norm_x, norm_mem, w_xq, w_xkv, w_xo, norm_moe, w_rg, b_rg, w_re,
           b_re, w_e_gate, w_e_up, w_e_down, norm_final):
    batch, seq, d = x.shape
    mem_len = mem.shape[1]
    t = batch * seq
    heads = dt_bias.shape[1]
    d_inner = heads * SSM_HEAD_DIM
    conv_dim = conv_w.shape[1]
    att_w = w_att_out.shape[1]
    att_heads = att_w // ATT_HEAD_DIM
    n_experts = w_re.shape[2]
    groups = w_rg.shape[2]
    per_group = n_experts // groups
    idx_heads = (w_in.shape[2] - (d_inner + conv_dim + heads + att_w + 2 * ATT_HEAD_DIM
                                  + IDX_HEAD_DIM + 2 * d)) // (IDX_HEAD_DIM + 1)
    idx_w = idx_heads * IDX_HEAD_DIM
    assert w_in.shape[0] == 1, "one layer; the final norm is fused into the layer's last call"
    assert seq % SSM_CHUNK == 0 and seq % Q_BLOCK == 0 and heads + idx_heads <= LANES
    assert groups + n_experts <= LANES and t % 1024 == 0 and heads % (2 * SSM_GROUPS) == 0
    li = 0

    sizes = (d_inner, conv_dim, heads, att_w, ATT_HEAD_DIM, ATT_HEAD_DIM, idx_w, IDX_HEAD_DIM,
             idx_heads, d, d)
    offs = [0]
    for s in sizes:
        offs.append(offs[-1] + s)
    wi_ = w_in[li]
    col = lambda k: wi_[:, offs[k]:offs[k + 1]]
    pad = lambda a, n: jnp.pad(a, ((0, 0), (0, n - a.shape[1])))
    lpad = lambda a, n: jnp.pad(a, ((0, 0), (n - a.shape[1], 0)))
    both = lambda a: [pad(a, LANES), lpad(a, LANES)]
    w_z = col(0).astype(BF16)
    w_xbc = col(1).astype(BF16)
    w_small = pad(jnp.concatenate([col(2), col(8)], axis=1), LANES).astype(BF16)
    w_qq = jnp.concatenate([col(3) * (LOG2E * ATT_HEAD_DIM ** -0.5), col(6)], axis=1).astype(BF16)
    w_kv = jnp.concatenate(both(col(7)) + [pad(col(4), LANES), pad(col(5), LANES)],
                           axis=1).astype(BF16)
    w_gates = jnp.concatenate([col(9), col(10)], axis=1).astype(BF16)

    h = x.reshape(t, d)
    u = _rmsnorm(h, norm_mix[li], BF16, 512)
    z = _matmul(u, w_z, BF16, 1024, 1024)
    xbc = _matmul(u, w_xbc, BF16, 1024, 1024)
    small = _matmul(u, w_small, F32, 512, LANES)
    qq = _matmul(u, w_qq, BF16, 512, att_w + idx_w)
    kvp = _matmul(u, w_kv, BF16, 512, 4 * LANES)
    gates = _matmul(u, w_gates, BF16, 1024, 1024)

    dtt = jnp.swapaxes(small[:, :heads].reshape(batch, seq, heads), 1, 2)
    ys = _ssd(z, xbc, small, dtt, conv_w[li], conv_b[li], dt_bias[li], a_log[li], d_skip[li],
              ssm_norm[li], batch, seq)
    wit = jnp.swapaxes(small[:, heads:heads + idx_heads].reshape(batch, seq, idx_heads), 1, 2)
    oa = _dsa(qq, kvp, wit, batch, seq, att_heads, idx_heads)

    x1, xn1 = _merge(h, ys, oa, gates, w_ssm_out[li].astype(BF16), w_att_out[li].astype(BF16),
                     w_mix_out[li].astype(BF16), norm_x[li], 512)

    mn = _rmsnorm(mem.reshape(batch * mem_len, d), norm_mem[li], BF16, 512)
    kvm = _matmul(mn, w_xkv[li].astype(BF16), BF16, 512, 1024)
    x2 = _xattn(x1, xn1, kvm, w_xq[li].astype(BF16), w_xo[li].astype(BF16), batch, seq,
                mem_len, 512)

    wr = pad(jnp.concatenate([w_rg[li], w_re[li]], axis=1), LANES)
    br = pad(jnp.concatenate([b_rg[li], b_re[li]]).reshape(1, -1), LANES)
    xm, route = _router(x2, norm_moe[li], wr, br, groups, per_group, 512)

    tile_expert, tile_valid, token, dest = _moe_plan(route, n_experts, EXPERT_TILE)
    ye = _experts(xm, tile_expert, tile_valid, token, dest, w_e_gate[li].astype(BF16),
                  w_e_up[li].astype(BF16), w_e_down[li].astype(BF16),
                  MOE_TOPK * t + 2 * EXPERT_TILE)
    out = _combine(x2, ye, route, norm_final, 512)
    return out.reshape(batch, seq, d)
```

```python
import functools
import math

import numpy as np
import jax
import jax.numpy as jnp
from jax import lax
from jax.experimental import pallas as pl
from jax.experimental.pallas import tpu as pltpu

F32 = jnp.float32
BF16 = jnp.bfloat16
I32 = jnp.int32
EPS = 1e-6
LOG2E = math.log2(math.e)

SSM_HEAD_DIM = 64
SSM_GROUPS = 4
SSM_STATE = 128
SSM_CHUNK = 128
ATT_HEAD_DIM = 64
IDX_HEAD_DIM = 64
TOPK_MAX = 256
Q_BLOCK = 128
NEG_INF = -1e30
X_HEADS = 4
MOE_GROUPS = 4
MOE_TOPK = 2

LANES = 128
VMEM_LIMIT = 56 * 1024 * 1024
EXPERT_TILE = 256
DSA_WIDTHS = 4
DSA_HEADS_PER_DOT = 4


def _cparams(*sem):
    return pltpu.CompilerParams(dimension_semantics=sem, vmem_limit_bytes=VMEM_LIMIT)


def _sigmoid(x):
    return 1.0 / (1.0 + jnp.exp(-x))


def _softplus(x):
    return jnp.maximum(x, 0.0) + jnp.log1p(jnp.exp(-jnp.abs(x)))


def _dot(a, b):
    return jnp.dot(a, b, preferred_element_type=F32)


def _split3(x):
    p0 = x.astype(BF16)
    r1 = x - p0.astype(F32)
    p1 = r1.astype(BF16)
    p2 = (r1 - p1.astype(F32)).astype(BF16)
    return p0, p1, p2


def _dot_exact_rhs(x, m):
    mb = m.astype(BF16)
    p0, p1, p2 = _split3(x)
    return _dot(p0, mb) + _dot(p1, mb) + _dot(p2, mb)


def _dot_exact_lhs(m, x):
    mb = m.astype(BF16)
    p0, p1, p2 = _split3(x)
    return _dot(mb, p0) + _dot(mb, p1) + _dot(mb, p2)


def _dot_nt(a, b):
    return lax.dot_general(a, b, (((1,), (1,)), ((), ())), preferred_element_type=F32)


def _reduce_rows(x, op, pair_op, chains=8):
    w, n = x.shape
    while w % (8 * chains):
        chains //= 2
    step = w // chains
    parts = [op(x[i * step:(i + 1) * step].reshape(step // 8, 8, n), axis=0) for i in range(chains)]
    while len(parts) > 1:
        parts = [pair_op(parts[i], parts[i + 1]) for i in range(0, len(parts), 2)]
    return op(parts[0], axis=0, keepdims=True)


def _rmsnorm_kernel(x_ref, g_ref, o_ref):
    x = x_ref[...]
    y = x * lax.rsqrt(jnp.mean(x * x, axis=-1, keepdims=True) + EPS) * g_ref[...]
    o_ref[...] = y.astype(o_ref.dtype)


def _rmsnorm(x, g, out_dtype, tm):
    m, d = x.shape
    return pl.pallas_call(
        _rmsnorm_kernel,
        grid=(m // tm,),
        in_specs=[pl.BlockSpec((tm, d), lambda i: (i, 0)),
                  pl.BlockSpec((1, d), lambda i: (0, 0))],
        out_specs=pl.BlockSpec((tm, d), lambda i: (i, 0)),
        out_shape=jax.ShapeDtypeStruct((m, d), out_dtype),
        compiler_params=_cparams("parallel"),
    )(x, g.reshape(1, d))


def _mm_kernel(a_ref, b_ref, o_ref):
    o_ref[...] = _dot(a_ref[...], b_ref[...]).astype(o_ref.dtype)


def _matmul(a, b, out_dtype, tm, tn):
    m, k = a.shape
    n = b.shape[1]
    return pl.pallas_call(
        _mm_kernel,
        grid=(n // tn, m // tm),
        in_specs=[pl.BlockSpec((tm, k), lambda j, i: (i, 0)),
                  pl.BlockSpec((k, tn), lambda j, i: (0, j))],
        out_specs=pl.BlockSpec((tm, tn), lambda j, i: (i, j)),
        out_shape=jax.ShapeDtypeStruct((m, n), out_dtype),
        compiler_params=_cparams("parallel", "parallel"),
    )(a, b)


def _ssd_kernel(z_ref, xbc_ref, dt_ref, dtt_ref, cw_ref, cb_ref, dtb_ref, dtbt_ref,
                alog_ref, alogt_ref, dskip_ref, norm_ref, e_ref, o_ref,
                tail_ref, st_ref, y_ref, *, heads, d_inner):
    q = SSM_CHUNK
    n = SSM_STATE
    hpg = heads // SSM_GROUPS
    gw = hpg * SSM_HEAD_DIM
    c = pl.program_id(1)

    @pl.when(c == 0)
    def _():
        tail_ref[...] = jnp.zeros_like(tail_ref)
        st_ref[...] = jnp.zeros_like(st_ref)

    x = xbc_ref[...]
    tl = tail_ref.shape[0]
    xcat = jnp.concatenate([tail_ref[...], x], axis=0)
    tail_ref[...] = x[q - tl:, :]
    cw = cw_ref[...]
    kconv = cw.shape[0]
    srow = lax.broadcasted_iota(I32, ((kconv - 1) * q, tl + q), 0)
    scol = lax.broadcasted_iota(I32, ((kconv - 1) * q, tl + q), 1)
    shift = jnp.where(scol - (tl - (kconv - 1)) == srow - (q - 1) * (srow // q), 1.0, 0.0)
    shifted = _dot(shift.astype(xcat.dtype), xcat)
    acc = cb_ref[...] + cw[kconv - 1:kconv, :] * x.astype(F32)
    for k in range(kconv - 1):
        acc = acc + cw[k:k + 1, :] * shifted[k * q:(k + 1) * q, :]
    xc = acc * _sigmoid(acc)
    xs = xc[:, :d_inner]
    bm = xc[:, d_inner:d_inner + SSM_GROUPS * n]
    cm = xc[:, d_inner + SSM_GROUPS * n:]

    dt = _softplus(dt_ref[...] + dtb_ref[...])
    da = dt * (-jnp.exp(alog_ref[...]))
    dtt = _softplus(dtt_ref[...] + dtbt_ref[...])
    dat = dtt * (-jnp.exp(alogt_ref[...]))
    rows = lax.broadcasted_iota(I32, (q, q), 0)
    cols = lax.broadcasted_iota(I32, (q, q), 1)
    causal = rows >= cols
    tril = jnp.where(causal, 1.0, 0.0)
    triu = jnp.where(rows <= cols, 1.0, 0.0)
    a_cs = _dot_exact_lhs(tril, da)
    a_cst = _dot_exact_rhs(dat, triu)
    expand = e_ref[...]
    dt_e = _dot_exact_rhs(dt, expand)
    acs_e = _dot_exact_rhs(a_cs, expand)
    expa = jnp.exp(acs_e)
    a_last = acs_e[q - 1:q, :]
    xdt = xs * dt_e
    xdt_b = xdt.astype(BF16)
    xdec_b = (xdt * jnp.exp(a_last - acs_e)).astype(BF16)
    st = st_ref[...]
    st_b = st.astype(BF16)
    lane = lax.broadcasted_iota(I32, (q, 2 * SSM_HEAD_DIM), 1)

    for g in range(SSM_GROUPS):
        bg = bm[:, g * n:(g + 1) * n]
        cg = cm[:, g * n:(g + 1) * n].astype(BF16)
        cb = _dot_nt(cg, bg.astype(BF16))
        gs = slice(g * gw, (g + 1) * gw)
        y_off = _dot(cg, st_b[:, gs])
        s_new = _dot(bg.T.astype(BF16), xdec_b[:, gs])
        st_ref[:, gs] = st[:, gs] * expa[q - 1:q, gs] + s_new
        for j in range(hpg // 2):
            h0 = g * hpg + 2 * j
            c0 = h0 * SSM_HEAD_DIM
            xp = xdt_b[:, c0:c0 + 2 * SSM_HEAD_DIM]
            parts = []
            for h in (h0, h0 + 1):
                seg = a_cs[:, h:h + 1] - a_cst[h:h + 1, :]
                lmat = jnp.exp(jnp.where(causal, seg, -jnp.inf))
                parts.append(_dot((cb * lmat).astype(BF16), xp))
            y_diag = jnp.where(lane < SSM_HEAD_DIM, parts[0], parts[1])
            cs = slice(c0, c0 + 2 * SSM_HEAD_DIM)
            y_ref[:, cs] = y_diag + y_off[:, c0 - g * gw:c0 - g * gw + 2 * SSM_HEAD_DIM] * expa[:, cs]

    y = y_ref[...] + dskip_ref[...] * xs
    zz = z_ref[...].astype(F32)
    yg = y * (zz * _sigmoid(zz))
    out = yg * lax.rsqrt(jnp.mean(yg * yg, axis=-1, keepdims=True) + EPS) * norm_ref[...]
    o_ref[...] = out.astype(o_ref.dtype)


def _ssd(z, xbc, small, dtt, conv_w, conv_b, dt_bias, a_log, d_skip, ssm_norm, batch, seq):
    heads = dt_bias.shape[0]
    d_inner = heads * SSM_HEAD_DIM
    conv_dim = xbc.shape[1]
    q = SSM_CHUNK
    nc = seq // q
    expand = jnp.repeat(jnp.eye(LANES, heads, dtype=BF16), SSM_HEAD_DIM, axis=1)
    lane_pad = lambda v: jnp.pad(v.reshape(1, -1), ((0, 0), (0, LANES - heads)))
    row = lambda b, c: (b * nc + c, 0)
    const = lambda b, c: (0, 0)
    kern = functools.partial(_ssd_kernel, heads=heads, d_inner=d_inner)
    return pl.pallas_call(
        kern,
        grid=(batch, nc),
        in_specs=[
            pl.BlockSpec((q, d_inner), row),
            pl.BlockSpec((q, conv_dim), row),
            pl.BlockSpec((q, LANES), row),
            pl.BlockSpec((None, heads, q), lambda b, c: (b, 0, c)),
            pl.BlockSpec(conv_w.T.shape, const),
            pl.BlockSpec((1, conv_dim), const),
            pl.BlockSpec((1, LANES), const),
            pl.BlockSpec((heads, 1), const),
            pl.BlockSpec((1, LANES), const),
            pl.BlockSpec((heads, 1), const),
            pl.BlockSpec((1, d_inner), const),
            pl.BlockSpec((1, d_inner), const),
            pl.BlockSpec((LANES, d_inner), const),
        ],
        out_specs=pl.BlockSpec((q, d_inner), row),
        out_shape=jax.ShapeDtypeStruct((batch * seq, d_inner), BF16),
        scratch_shapes=[pltpu.VMEM((16, conv_dim), xbc.dtype),
                        pltpu.VMEM((SSM_STATE, d_inner), F32),
                        pltpu.VMEM((q, d_inner), F32)],
        compiler_params=_cparams("parallel", "arbitrary"),
    )(z, xbc, small, dtt, conv_w.T, conv_b.reshape(1, -1),
      lane_pad(dt_bias), dt_bias.reshape(-1, 1), lane_pad(a_log), a_log.reshape(-1, 1),
      jnp.repeat(d_skip, SSM_HEAD_DIM).reshape(1, -1), ssm_norm.reshape(1, -1), expand)


def _order_key(x):
    bits = int(np.array(x, np.float32).view(np.int32))
    return bits ^ ((bits >> 31) & 0x7FFFFFFF)


def _dsa_consts(seq, att_heads):
    hd = ATT_HEAD_DIM
    assert seq <= 16 * 256 and 2 * hd == LANES
    qc = np.zeros((att_heads, LANES), np.float32)
    for h in range(att_heads):
        rest = np.float32(2.0 ** (-8.0 * (h + 1) / att_heads) * LOG2E)
        for i in range(3):
            piece = np.float32(rest.astype(jnp.bfloat16))
            rest = np.float32(rest - piece)
            qc[h, hd + i] = 16.0 * piece
            qc[h, hd + 3 + i] = piece
    pos = np.arange(seq)
    kpos = np.zeros((seq, LANES), np.float32)
    kpos[:, hd:hd + 3] = (pos // 16)[:, None]
    kpos[:, hd + 3:hd + 6] = (pos % 16)[:, None]
    return jnp.asarray(qc), jnp.asarray(kpos, dtype=BF16)


def _dsa_kernel(q_ref, qi_ref, ka_ref, kip_ref, vt_ref, wit_ref, qc_ref, o_ref,
                key_ref, mb_ref, qa_ref, *, topk, att_heads, idx_heads, widths):
    qb = Q_BLOCK
    seq = ka_ref.shape[0]
    nq = seq // qb
    blk = pl.program_id(1)
    hd = ATT_HEAD_DIM
    kf = float(topk)
    key_ni = _order_key(NEG_INF)
    left = lax.broadcasted_iota(I32, (qb, LANES), 1) < hd
    tpos = blk * qb + lax.broadcasted_iota(I32, (1, qb), 1)

    hpd = DSA_HEADS_PER_DOT
    for h in range(att_heads):
        qpair = q_ref[:, (h // 2) * LANES:(h // 2 + 1) * LANES].astype(F32)
        qh = qpair if h % 2 == 0 else pltpu.roll(qpair, hd, axis=1)
        qa_ref[h // hpd, (h % hpd) * qb:(h % hpd + 1) * qb, :] = jnp.where(
            left, qh, qc_ref[h:h + 1, :]).astype(BF16)

    def body(w):
        n_out = float(seq - w)
        spos = lax.broadcasted_iota(I32, (w, 1), 0)
        causal = spos <= tpos

        wit = wit_ref[...] * (idx_heads ** -0.5 * IDX_HEAD_DIM ** -0.5)
        isc = jnp.zeros((w, qb), F32)
        qi_rows = jnp.concatenate([qi_ref[:, g * LANES:(g + 1) * LANES]
                                   for g in range(idx_heads // 2)], axis=0)
        for par in range(2):
            rel = _dot_nt(kip_ref[:w, par * LANES:(par + 1) * LANES], qi_rows)
            for g in range(idx_heads // 2):
                h = 2 * g + par
                isc = isc + jnp.maximum(rel[:, g * qb:(g + 1) * qb], 0.0) * wit[h:h + 1, :]
        masked = jnp.where(causal, isc, NEG_INF) + 0.0
        bits = pltpu.bitcast(masked, I32)
        key_ref[:w, :] = bits ^ ((bits >> 31) & 0x7FFFFFFF)

        def count(mask):
            return _reduce_rows(jnp.where(mask, 1.0, 0.0), jnp.sum, jnp.add)

        def count_ge(cand):
            return count(key_ref[:w, :] >= cand) + jnp.where(key_ni >= cand, n_out, 0.0)

        int_min = jnp.full((1, qb), -2 ** 31, I32)
        thr0 = jnp.where(count_ge(jnp.zeros((1, qb), I32)) >= kf, 0, int_min)

        def thr_body(j, thr):
            cand = thr + lax.shift_left(jnp.int32(1), 30 - j)
            return jnp.where(count_ge(cand) >= kf, cand, thr)

        thr = lax.fori_loop(0, 31, thr_body, thr0)

        key = key_ref[:w, :]
        gt = key > thr
        eqc = (key == thr) & causal
        need = kf - count(gt) - jnp.where(key_ni > thr, n_out, 0.0)
        has_tie = jnp.max(jnp.where(count(eqc) > need, 1.0, 0.0)) > 0.0

        @pl.when(jnp.logical_not(has_tie))
        def _():
            mb_ref[:w, :] = jnp.where(causal & (key >= thr), 0.0, -jnp.inf)

        @pl.when(has_tie)
        def _():
            nbits = max(1, (w - 1).bit_length())

            def cut_body(b, cut):
                cand = cut + lax.shift_left(jnp.int32(1), nbits - 1 - b)
                eq_here = (key_ref[:w, :] == thr) & causal
                return jnp.where(count(eq_here & (spos < cand)) < need, cand, cut)

            cut = lax.fori_loop(0, nbits, cut_body, jnp.zeros((1, qb), I32))
            sel = gt | (eqc & (spos <= cut))
            mb_ref[:w, :] = jnp.where(causal & sel, 0.0, -jnp.inf)

        n_dots = att_heads // hpd
        s_next = _dot_nt(ka_ref[:w, :], qa_ref[0])
        for jd in range(n_dots):
            s = s_next
            if jd + 1 < n_dots:
                s_next = _dot_nt(ka_ref[:w, :], qa_ref[jd + 1])
            ps = []
            for e in range(hpd):
                sh = s[:, e * qb:(e + 1) * qb] + mb_ref[:w, :]
                ps.append(jnp.exp2((sh - _reduce_rows(sh, jnp.max, jnp.maximum)).astype(BF16)))
            acc = _dot(vt_ref[:, :w], jnp.concatenate(ps, axis=1))
            o = acc * (1.0 / acc[hd:hd + 1, :])
            for e in range(0, hpd, 2):
                pair = jnp.where(left, o[:, e * qb:(e + 1) * qb].T,
                                 pltpu.roll(o[:, (e + 1) * qb:(e + 2) * qb].T, hd, axis=1))
                g = (jd * hpd + e) // 2
                o_ref[:, g * LANES:(g + 1) * LANES] = pair.astype(o_ref.dtype)

    per = nq // widths
    for v in range(widths):
        @pl.when((blk >= v * per) & (blk < (v + 1) * per))
        def _(v=v):
            body((v + 1) * per * qb)


def _dsa(qq, kvp, wit, batch, seq, att_heads, idx_heads):
    qb = Q_BLOCK
    nb = seq // qb
    hd = ATT_HEAD_DIM
    att_w = att_heads * hd
    idx_w = idx_heads * IDX_HEAD_DIM
    widths = min(DSA_WIDTHS, nb)
    assert att_w % idx_w == 0 and nb % widths == 0 and idx_heads % 2 == 0
    assert att_heads % DSA_HEADS_PER_DOT == 0 and DSA_HEADS_PER_DOT % 2 == 0
    assert 2 * IDX_HEAD_DIM == LANES and qb == LANES
    topk = min(TOPK_MAX, seq // 4)
    qc, kpos = _dsa_consts(seq, att_heads)
    ka = kvp[:, 2 * LANES:3 * LANES] + jnp.tile(kpos, (batch, 1))
    vt = jnp.swapaxes(kvp[:, 3 * LANES:].reshape(batch, seq, LANES), 1, 2)
    vt = vt.at[:, hd, :].set(1.0)
    kern = functools.partial(_dsa_kernel, topk=topk, att_heads=att_heads, idx_heads=idx_heads,
                             widths=widths)
    return pl.pallas_call(
        kern,
        grid=(batch, nb),
        in_specs=[
            pl.BlockSpec((qb, att_w), lambda b, i: (b * nb + i, 0)),
            pl.BlockSpec((qb, idx_w), lambda b, i: (b * nb + i, att_w // idx_w)),
            pl.BlockSpec((seq, LANES), lambda b, i: (b, 0)),
            pl.BlockSpec((seq, 2 * LANES), lambda b, i: (b, 0)),
            pl.BlockSpec((None, LANES, seq), lambda b, i: (b, 0, 0)),
            pl.BlockSpec((None, idx_heads, qb), lambda b, i: (b, 0, i)),
            pl.BlockSpec(qc.shape, lambda b, i: (0, 0)),
        ],
        out_specs=pl.BlockSpec((qb, att_w), lambda b, i: (b * nb + i, 0)),
        out_shape=jax.ShapeDtypeStruct((batch * seq, att_w), BF16),
        scratch_shapes=[pltpu.VMEM((seq, qb), I32),
                        pltpu.VMEM((seq, qb), F32),
                        pltpu.VMEM((att_heads // DSA_HEADS_PER_DOT, DSA_HEADS_PER_DOT * qb, LANES),
                                   BF16)],
        compiler_params=_cparams("parallel", "parallel"),
    )(qq, qq, ka, kvp, vt, wit, qc)


def _merge_kernel(x_ref, ys_ref, oa_ref, g_ref, wso_ref, wao_ref, wmo_ref, nx_ref,
                  x1_ref, xn_ref):
    d = x_ref.shape[1]
    y_ssm = _dot(ys_ref[...], wso_ref[...])
    y_att = _dot(oa_ref[...], wao_ref[...])
    g = g_ref[...].astype(F32)
    mix = _sigmoid(g[:, :d]) * y_ssm + _sigmoid(g[:, d:]) * y_att
    x1 = x_ref[...] + _dot(mix.astype(BF16), wmo_ref[...])
    x1_ref[...] = x1
    xn = x1 * lax.rsqrt(jnp.mean(x1 * x1, axis=-1, keepdims=True) + EPS) * nx_ref[...]
    xn_ref[...] = xn.astype(xn_ref.dtype)


def _merge(x, ys, oa, gates, wso, wao, wmo, norm_x, tm):
    t, d = x.shape
    row = lambda i: (i, 0)
    const = lambda i: (0, 0)
    return pl.pallas_call(
        _merge_kernel,
        grid=(t // tm,),
        in_specs=[pl.BlockSpec((tm, d), row),
                  pl.BlockSpec((tm, ys.shape[1]), row),
                  pl.BlockSpec((tm, oa.shape[1]), row),
                  pl.BlockSpec((tm, 2 * d), row),
                  pl.BlockSpec(wso.shape, const),
                  pl.BlockSpec(wao.shape, const),
                  pl.BlockSpec(wmo.shape, const),
                  pl.BlockSpec((1, d), const)],
        out_specs=[pl.BlockSpec((tm, d), row), pl.BlockSpec((tm, d), row)],
        out_shape=[jax.ShapeDtypeStruct((t, d), F32), jax.ShapeDtypeStruct((t, d), BF16)],
        compiler_params=_cparams("parallel"),
    )(x, ys, oa, gates, wso, wao, wmo, norm_x.reshape(1, d))


def _xattn_kernel(x1_ref, xn_ref, kv_ref, wq_ref, wo_ref, x2_ref):
    d = x1_ref.shape[1]
    hd = d // X_HEADS
    qf = _dot(xn_ref[...], wq_ref[...]) * (hd ** -0.5)
    qb = qf.astype(BF16)
    outs = []
    for h in range(X_HEADS):
        kh = kv_ref[:, h * hd:(h + 1) * hd]
        vh = kv_ref[:, d + h * hd:d + (h + 1) * hd]
        s = _dot_nt(qb[:, h * hd:(h + 1) * hd], kh)
        m = jnp.max(s, axis=-1, keepdims=True)
        p = jnp.exp(s - m)
        l = jnp.sum(p, axis=-1, keepdims=True)
        outs.append((_dot(p.astype(BF16), vh) / l).astype(BF16))
    o = jnp.concatenate(outs, axis=-1)
    x2_ref[...] = x1_ref[...] + _dot(o, wo_ref[...])


def _xattn(x1, xn, kvm, wq, wo, batch, seq, mem_len, tm):
    t, d = x1.shape
    nt = seq // tm
    row = lambda b, i: (b * nt + i, 0)
    const = lambda b, i: (0, 0)
    return pl.pallas_call(
        _xattn_kernel,
        grid=(batch, nt),
        in_specs=[pl.BlockSpec((tm, d), row),
                  pl.BlockSpec((tm, d), row),
                  pl.BlockSpec((mem_len, 2 * d), lambda b, i: (b, 0)),
                  pl.BlockSpec(wq.shape, const),
                  pl.BlockSpec(wo.shape, const)],
        out_specs=pl.BlockSpec((tm, d), row),
        out_shape=jax.ShapeDtypeStruct((t, d), F32),
        compiler_params=_cparams("parallel", "parallel"),
    )(x1, xn, kvm, wq, wo)


def _router_kernel(x_ref, g_ref, wr_ref, br_ref, xm_ref, route_ref, *, groups, per_group):
    x = x_ref[...]
    xm = x * lax.rsqrt(jnp.mean(x * x, axis=-1, keepdims=True) + EPS) * g_ref[...]
    xm_ref[...] = xm
    x0, x1, _ = _split3(xm)
    w0, w1, _ = _split3(wr_ref[...])
    logits = _dot(x0, w0) + _dot(x0, w1) + _dot(x1, w0) + br_ref[...]
    lane = lax.broadcasted_iota(I32, logits.shape, 1)
    lanef = lane.astype(F32)
    far = float(LANES)
    ninf = -jnp.inf

    gl = jnp.where(lane < groups, logits, ninf)
    gmax = jnp.max(gl, axis=-1, keepdims=True)
    gsel = jnp.min(jnp.where(gl == gmax, lanef, far), axis=-1, keepdims=True)
    ge = jnp.exp(gl - gmax)
    pg = jnp.max(ge / jnp.sum(ge, axis=-1, keepdims=True), axis=-1, keepdims=True)

    lo = groups + per_group * gsel
    em = (lanef >= lo) & (lanef < lo + per_group)
    el = jnp.where(em, logits, ninf)
    ee = jnp.exp(el - jnp.max(el, axis=-1, keepdims=True))
    ep = jnp.where(em, ee / jnp.sum(ee, axis=-1, keepdims=True), -1.0)
    p1 = jnp.max(ep, axis=-1, keepdims=True)
    i1 = jnp.min(jnp.where(ep == p1, lanef, far), axis=-1, keepdims=True)
    ep2 = jnp.where(lanef == i1, -1.0, ep)
    p2 = jnp.max(ep2, axis=-1, keepdims=True)
    i2 = jnp.min(jnp.where(ep2 == p2, lanef, far), axis=-1, keepdims=True)
    w1 = pg * p1 / (p1 + p2)
    w2 = pg * p2 / (p1 + p2)
    route_ref[...] = jnp.where(lane == 0, i1 - groups,
                     jnp.where(lane == 1, i2 - groups,
                     jnp.where(lane == 2, w1,
                     jnp.where(lane == 3, w2, 0.0))))


def _router(x2, norm_moe, wr, br, groups, per_group, tm):
    t, d = x2.shape
    row = lambda i: (i, 0)
    const = lambda i: (0, 0)
    kern = functools.partial(_router_kernel, groups=groups, per_group=per_group)
    return pl.pallas_call(
        kern,
        grid=(t // tm,),
        in_specs=[pl.BlockSpec((tm, d), row), pl.BlockSpec((1, d), const),
                  pl.BlockSpec((d, LANES), const), pl.BlockSpec((1, LANES), const)],
        out_specs=[pl.BlockSpec((tm, d), row), pl.BlockSpec((tm, LANES), row)],
        out_shape=[jax.ShapeDtypeStruct((t, d), F32), jax.ShapeDtypeStruct((t, LANES), F32)],
        compiler_params=_cparams("parallel"),
    )(x2, norm_moe.reshape(1, d), wr, br)


def _expert_kernel(te_ref, tv_ref, gcur_ref, gnext_ref, sidx_ref, xm_ref, wg_ref, wu_ref, wd_ref,
                   y_ref, xbuf, ybuf, wgb, wub, wdb, gsem, ssem):
    i = pl.program_id(0)
    nt = pl.num_programs(0)
    slot = i % 2
    groups8 = xbuf.shape[1]
    rows = 8 * groups8
    d = xbuf.shape[3]
    nxt = jnp.minimum(i + 1, nt - 1)
    valid = tv_ref[i] > 0
    has_next = (i + 1 < nt) & (tv_ref[nxt] > 0)

    def gather_copy(src_row, s, g8, u):
        return pltpu.make_async_copy(xm_ref.at[pl.ds(src_row, 1), :],
                                     xbuf.at[s, g8, pl.ds(u, 1), :], gsem.at[s])

    def scatter_copy(s, g8, u, dst_row):
        return pltpu.make_async_copy(ybuf.at[s, g8, pl.ds(u, 1), :],
                                     y_ref.at[pl.ds(dst_row, 1), :], ssem.at[s])

    def per_row(fn):
        def g8_body(g8, carry):
            for u in range(8):
                fn(g8, u)
            return carry
        lax.fori_loop(0, groups8, g8_body, 0)

    def gather_start(idx_ref, s):
        per_row(lambda g8, u: gather_copy(idx_ref[0, g8 * 8 + u], s, g8, u).start(priority=u % 2))

    def gather_wait(s):
        per_row(lambda g8, u: gather_copy(0, s, g8, u).wait())

    def scatter_start(s):
        per_row(lambda g8, u: scatter_copy(s, g8, u, sidx_ref[0, g8 * 8 + u]).start(priority=u % 2))

    def scatter_wait(s):
        per_row(lambda g8, u: scatter_copy(s, g8, u, 0).wait())

    @pl.when(i == 0)
    def _():
        ybuf[1] = jnp.zeros(ybuf.shape[1:], ybuf.dtype)
        n_real = y_ref.shape[0] - 2 * rows

        def fill_copy(g):
            return pltpu.make_async_copy(ybuf.at[1, g % groups8],
                                         y_ref.at[pl.ds(n_real + g * 8, 8), :], ssem.at[1])

        def fill_start(g, carry):
            fill_copy(g).start()
            return carry

        def fill_wait(g, carry):
            fill_copy(g).wait()
            return carry

        lax.fori_loop(0, 2 * groups8, fill_start, 0)
        lax.fori_loop(0, 2 * groups8, fill_wait, 0)

    @pl.when((i == 0) & valid)
    def _():
        gather_start(gcur_ref, 0)

    @pl.when(valid & ((i == 0) | (te_ref[i] != te_ref[jnp.maximum(i - 1, 0)])))
    def _():
        wgb[...] = wg_ref[...].astype(BF16)
        wub[...] = wu_ref[...].astype(BF16)
        wdb[...] = wd_ref[...].astype(BF16)

    @pl.when(valid)
    def _():
        @pl.when(has_next)
        def _():
            gather_start(gnext_ref, 1 - slot)

        gather_wait(slot)

        @pl.when(i >= 2)
        def _():
            scatter_wait(slot)

        xb = xbuf[slot].reshape(rows, d).astype(BF16)
        gate = _dot(xb, wgb[...])
        hid = gate * _sigmoid(gate) * _dot(xb, wub[...])
        ybuf[slot] = _dot(hid.astype(BF16), wdb[...]).reshape(groups8, 8, d)
        scatter_start(slot)

        @pl.when(jnp.logical_not(has_next))
        def _():
            scatter_wait(slot)

            @pl.when(i >= 1)
            def _():
                scatter_wait(1 - slot)


def _experts(xm, tile_expert, tile_valid, gidx, sidx, wg, wu, wd, n_out):
    t, d = xm.shape
    tm = EXPERT_TILE
    nt = tile_expert.shape[0]
    hid = wg.shape[2]
    smem_rows = lambda imap: pl.BlockSpec((None, 1, tm), imap, memory_space=pltpu.SMEM)
    wspec = lambda shape: pl.BlockSpec((None,) + shape, lambda i, te, tv: (te[i], 0, 0))
    grid_spec = pltpu.PrefetchScalarGridSpec(
        num_scalar_prefetch=2,
        grid=(nt,),
        in_specs=[smem_rows(lambda i, te, tv: (i, 0, 0)),
                  smem_rows(lambda i, te, tv: (jnp.minimum(i + 1, nt - 1), 0, 0)),
                  smem_rows(lambda i, te, tv: (i, 0, 0)),
                  pl.BlockSpec(memory_space=pl.ANY),
                  wspec((d, hid)), wspec((d, hid)), wspec((hid, d))],
        out_specs=pl.BlockSpec(memory_space=pl.ANY),
        scratch_shapes=[pltpu.VMEM((2, tm // 8, 8, d), F32), pltpu.VMEM((2, tm // 8, 8, d), F32),
                        pltpu.VMEM((d, hid), BF16), pltpu.VMEM((d, hid), BF16),
                        pltpu.VMEM((hid, d), BF16),
                        pltpu.SemaphoreType.DMA((2,)), pltpu.SemaphoreType.DMA((2,))],
    )
    g3 = gidx.reshape(nt, 1, tm)
    return pl.pallas_call(
        _expert_kernel,
        grid_spec=grid_spec,
        out_shape=jax.ShapeDtypeStruct((n_out, d), F32),
        compiler_params=_cparams("arbitrary"),
    )(tile_expert, tile_valid, g3, g3, sidx.reshape(nt, 1, tm), xm, wg, wu, wd)


def _combine_kernel(x_ref, y1_ref, y2_ref, route_ref, g_ref, o_ref):
    w1 = route_ref[:, 2:3]
    w2 = route_ref[:, 3:4]
    x = x_ref[...] + w1 * y1_ref[...] + w2 * y2_ref[...]
    o_ref[...] = x * lax.rsqrt(jnp.mean(x * x, axis=-1, keepdims=True) + EPS) * g_ref[...]


def _combine(x2, y, route, norm_final, tm):
    t, d = x2.shape
    row = lambda i: (i, 0)
    return pl.pallas_call(
        _combine_kernel,
        grid=(t // tm,),
        in_specs=[pl.BlockSpec((tm, d), row), pl.BlockSpec((tm, d), row),
                  pl.BlockSpec((tm, d), lambda i: (i + t // tm, 0)),
                  pl.BlockSpec((tm, LANES), row), pl.BlockSpec((1, d), lambda i: (0, 0))],
        out_specs=pl.BlockSpec((tm, d), row),
        out_shape=jax.ShapeDtypeStruct((t, d), F32),
        compiler_params=_cparams("parallel"),
    )(x2, y, y, route, norm_final.reshape(1, d))


def _moe_plan(route, n_experts, tile):
    t = route.shape[0]
    pairs = MOE_TOPK * t
    n_tiles = pairs // tile + n_experts
    e_flat = route[:, :MOE_TOPK].astype(I32).reshape(pairs)
    order = jnp.argsort(e_flat, stable=True).astype(I32)
    counts = jnp.sum((e_flat[:, None] == jnp.arange(n_experts, dtype=I32)[None, :]).astype(I32), axis=0)
    tiles_per = (counts + tile - 1) // tile
    tile_end = jnp.cumsum(tiles_per)
    sorted_start = jnp.cumsum(counts) - counts
    tile_ids = jnp.arange(n_tiles, dtype=I32)
    tile_valid = (tile_ids < tile_end[-1]).astype(I32)
    tile_expert = jnp.minimum(jnp.sum((tile_ids[:, None] >= tile_end[None, :]).astype(I32), axis=1),
                              n_experts - 1)
    last_expert = tile_expert[jnp.maximum(tile_end[-1] - 1, 0)]
    tile_expert = jnp.where(tile_valid > 0, tile_expert, last_expert)
    tile_first = (tile_end - tiles_per)[tile_expert]
    lane_row = jnp.arange(tile, dtype=I32)[None, :]
    row_in_group = ((tile_ids - tile_first) * tile)[:, None] + lane_row
    row_valid = (row_in_group < counts[tile_expert][:, None]) & (tile_valid[:, None] > 0)
    src = jnp.clip(sorted_start[tile_expert][:, None] + row_in_group, 0, pairs - 1)
    pair = order[src]
    token = jnp.where(row_valid, pair // MOE_TOPK, 0)
    spare = pairs + (tile_ids % 2)[:, None] * tile + lane_row
    dest = jnp.where(row_valid, (pair % MOE_TOPK) * t + pair // MOE_TOPK, spare)
    return tile_expert, tile_valid, token.reshape(-1), dest.reshape(-1)


def kernel(x, mem, norm_mix, w_in, conv_w, conv_b, dt_bias, a_log, d_skip, ssm_norm, w_ssm_out,
           w_att_out, w_mix_out, norm_x, norm_mem, w_xq, w_xkv, w_xo, norm_moe, w_rg, b_rg, w_re,
           b_re, w_e_gate, w_e_up, w_e_down, norm_final):
    batch, seq, d = x.shape
    mem_len = mem.shape[1]
    t = batch * seq
    heads = dt_bias.shape[1]
    d_inner = heads * SSM_HEAD_DIM
    conv_dim = conv_w.shape[1]
    att_w = w_att_out.shape[1]
    att_heads = att_w // ATT_HEAD_DIM
    n_experts = w_re.shape[2]
    groups = w_rg.shape[2]
    per_group = n_experts // groups
    idx_heads = (w_in.shape[2] - (d_inner + conv_dim + heads + att_w + 2 * ATT_HEAD_DIM
                                  + IDX_HEAD_DIM + 2 * d)) // (IDX_HEAD_DIM + 1)
    idx_w = idx_heads * IDX_HEAD_DIM
    assert w_in.shape[0] == 1, "one layer; the final norm is fused into the layer's last call"
    assert seq % SSM_CHUNK == 0 and seq % Q_BLOCK == 0 and heads + idx_heads <= LANES
    assert groups + n_experts <= LANES and t % 1024 == 0 and heads % (2 * SSM_GROUPS) == 0
    li = 0

    sizes = (d_inner, conv_dim, heads, att_w, ATT_HEAD_DIM, ATT_HEAD_DIM, idx_w, IDX_HEAD_DIM,
             idx_heads, d, d)
    offs = [0]
    for s in sizes:
        offs.append(offs[-1] + s)
    wi_ = w_in[li]
    col = lambda k: wi_[:, offs[k]:offs[k + 1]]
    pad = lambda a, n: jnp.pad(a, ((0, 0), (0, n - a.shape[1])))
    lpad = lambda a, n: jnp.pad(a, ((0, 0), (n - a.shape[1], 0)))
    both = lambda a: [pad(a, LANES), lpad(a, LANES)]
    w_z = col(0).astype(BF16)
    w_xbc = col(1).astype(BF16)
    w_small = pad(jnp.concatenate([col(2), col(8)], axis=1), LANES).astype(BF16)
    w_qq = jnp.concatenate([col(3) * (LOG2E * ATT_HEAD_DIM ** -0.5), col(6)], axis=1).astype(BF16)
    w_kv = jnp.concatenate(both(col(7)) + [pad(col(4), LANES), pad(col(5), LANES)],
                           axis=1).astype(BF16)
    w_gates = jnp.concatenate([col(9), col(10)], axis=1).astype(BF16)

    h = x.reshape(t, d)
    u = _rmsnorm(h, norm_mix[li], BF16, 512)
    z = _matmul(u, w_z, BF16, 1024, 1024)
    xbc = _matmul(u, w_xbc, BF16, 1024, 1024)
    small = _matmul(u, w_small, F32, 512, LANES)
    qq = _matmul(u, w_qq, BF16, 512, att_w + idx_w)
    kvp = _matmul(u, w_kv, BF16, 512, 4 * LANES)
    gates = _matmul(u, w_gates, BF16, 1024, 1024)

    dtt = jnp.swapaxes(small[:, :heads].reshape(batch, seq, heads), 1, 2)
    ys = _ssd(z, xbc, small, dtt, conv_w[li], conv_b[li], dt_bias[li], a_log[li], d_skip[li],
              ssm_norm[li], batch, seq)
    wit = jnp.swapaxes(small[:, heads:heads + idx_heads].reshape(batch, seq, idx_heads), 1, 2)
    oa = _dsa(qq, kvp, wit, batch, seq, att_heads, idx_heads)

    x1, xn1 = _merge(h, ys, oa, gates, w_ssm_out[li].astype(BF16), w_att_out[li].astype(BF16),
                     w_mix_out[li].astype(BF16), norm_x[li], 512)

    mn = _rmsnorm(mem.reshape(batch * mem_len, d), norm_mem[li], BF16, 512)
    kvm = _matmul(mn, w_xkv[li].astype(BF16), BF16, 512, 1024)
    x2 = _xattn(x1, xn1, kvm, w_xq[li].astype(BF16), w_xo[li].astype(BF16), batch, seq,
                mem_len, 512)

    wr = pad(jnp.concatenate([w_rg[li], w_re[li]], axis=1), LANES)
    br = pad(jnp.concatenate([b_rg[li], b_re[li]]).reshape(1, -1), LANES)
    xm, route = _router(x2, norm_moe[li], wr, br, groups, per_group, 512)

    tile_expert, tile_valid, token, dest = _moe_plan(route, n_experts, EXPERT_TILE)
    ye = _experts(xm, tile_expert, tile_valid, token, dest, w_e_gate[li], w_e_up[li], w_e_down[li],
                  MOE_TOPK * t + 2 * EXPERT_TILE)
    out = _combine(x2, ye, route, norm_final, 512)
    return out.reshape(batch, seq, d)
```

```python
import functools
import math

import numpy as np
import jax
import jax.numpy as jnp
from jax import lax
from jax.experimental import pallas as pl
from jax.experimental.pallas import tpu as pltpu

F32 = jnp.float32
BF16 = jnp.bfloat16
I32 = jnp.int32
EPS = 1e-6
LOG2E = math.log2(math.e)

SSM_HEAD_DIM = 64
SSM_GROUPS = 4
SSM_STATE = 128
SSM_CHUNK = 128
ATT_HEAD_DIM = 64
IDX_HEAD_DIM = 64
TOPK_MAX = 256
Q_BLOCK = 128
NEG_INF = -1e30
X_HEADS = 4
MOE_GROUPS = 4
MOE_TOPK = 2

LANES = 128
VMEM_LIMIT = 56 * 1024 * 1024
EXPERT_TILE = 256
DSA_WIDTHS = 4
DSA_HEADS_PER_DOT = 4


def _cparams(*sem):
    return pltpu.CompilerParams(dimension_semantics=sem, vmem_limit_bytes=VMEM_LIMIT)


def _sigmoid(x):
    return 1.0 / (1.0 + jnp.exp(-x))


def _softplus(x):
    return jnp.maximum(x, 0.0) + jnp.log1p(jnp.exp(-jnp.abs(x)))


def _dot(a, b):
    return jnp.dot(a, b, preferred_element_type=F32)


def _split3(x):
    p0 = x.astype(BF16)
    r1 = x - p0.astype(F32)
    p1 = r1.astype(BF16)
    p2 = (r1 - p1.astype(F32)).astype(BF16)
    return p0, p1, p2


def _dot_exact_rhs(x, m):
    mb = m.astype(BF16)
    p0, p1, p2 = _split3(x)
    return _dot(p0, mb) + _dot(p1, mb) + _dot(p2, mb)


def _dot_exact_lhs(m, x):
    mb = m.astype(BF16)
    p0, p1, p2 = _split3(x)
    return _dot(mb, p0) + _dot(mb, p1) + _dot(mb, p2)


def _dot_nt(a, b):
    return lax.dot_general(a, b, (((1,), (1,)), ((), ())), preferred_element_type=F32)


def _reduce_rows(x, op, pair_op, chains=8):
    w, n = x.shape
    while w % (8 * chains):
        chains //= 2
    step = w // chains
    parts = [op(x[i * step:(i + 1) * step].reshape(step // 8, 8, n), axis=0) for i in range(chains)]
    while len(parts) > 1:
        parts = [pair_op(parts[i], parts[i + 1]) for i in range(0, len(parts), 2)]
    return op(parts[0], axis=0, keepdims=True)


def _rmsnorm_kernel(x_ref, g_ref, o_ref):
    x = x_ref[...]
    y = x * lax.rsqrt(jnp.mean(x * x, axis=-1, keepdims=True) + EPS) * g_ref[...]
    o_ref[...] = y.astype(o_ref.dtype)


def _rmsnorm(x, g, out_dtype, tm):
    m, d = x.shape
    return pl.pallas_call(
        _rmsnorm_kernel,
        grid=(m // tm,),
        in_specs=[pl.BlockSpec((tm, d), lambda i: (i, 0)),
                  pl.BlockSpec((1, d), lambda i: (0, 0))],
        out_specs=pl.BlockSpec((tm, d), lambda i: (i, 0)),
        out_shape=jax.ShapeDtypeStruct((m, d), out_dtype),
        compiler_params=_cparams("parallel"),
    )(x, g.reshape(1, d))


def _mm_kernel(a_ref, b_ref, o_ref):
    o_ref[...] = _dot(a_ref[...], b_ref[...]).astype(o_ref.dtype)


def _matmul(a, b, out_dtype, tm, tn):
    m, k = a.shape
    n = b.shape[1]
    return pl.pallas_call(
        _mm_kernel,
        grid=(n // tn, m // tm),
        in_specs=[pl.BlockSpec((tm, k), lambda j, i: (i, 0)),
                  pl.BlockSpec((k, tn), lambda j, i: (0, j))],
        out_specs=pl.BlockSpec((tm, tn), lambda j, i: (i, j)),
        out_shape=jax.ShapeDtypeStruct((m, n), out_dtype),
        compiler_params=_cparams("parallel", "parallel"),
    )(a, b)


def _ssd_kernel(z_ref, xbc_ref, dt_ref, dtt_ref, cw_ref, cb_ref, dtb_ref, dtbt_ref,
                alog_ref, alogt_ref, dskip_ref, norm_ref, e_ref, o_ref,
                tail_ref, st_ref, y_ref, *, heads, d_inner):
    q = SSM_CHUNK
    n = SSM_STATE
    hpg = heads // SSM_GROUPS
    gw = hpg * SSM_HEAD_DIM
    c = pl.program_id(1)

    @pl.when(c == 0)
    def _():
        tail_ref[...] = jnp.zeros_like(tail_ref)
        st_ref[...] = jnp.zeros_like(st_ref)

    x = xbc_ref[...]
    tl = tail_ref.shape[0]
    xcat = jnp.concatenate([tail_ref[...], x], axis=0)
    tail_ref[...] = x[q - tl:, :]
    cw = cw_ref[...]
    kconv = cw.shape[0]
    srow = lax.broadcasted_iota(I32, ((kconv - 1) * q, tl + q), 0)
    scol = lax.broadcasted_iota(I32, ((kconv - 1) * q, tl + q), 1)
    shift = jnp.where(scol - (tl - (kconv - 1)) == srow - (q - 1) * (srow // q), 1.0, 0.0)
    shifted = _dot(shift.astype(xcat.dtype), xcat)
    acc = cb_ref[...] + cw[kconv - 1:kconv, :] * x.astype(F32)
    for k in range(kconv - 1):
        acc = acc + cw[k:k + 1, :] * shifted[k * q:(k + 1) * q, :]
    xc = acc * _sigmoid(acc)
    xs = xc[:, :d_inner]
    bm = xc[:, d_inner:d_inner + SSM_GROUPS * n]
    cm = xc[:, d_inner + SSM_GROUPS * n:]

    dt = _softplus(dt_ref[...] + dtb_ref[...])
    da = dt * (-jnp.exp(alog_ref[...]))
    dtt = _softplus(dtt_ref[...] + dtbt_ref[...])
    dat = dtt * (-jnp.exp(alogt_ref[...]))
    rows = lax.broadcasted_iota(I32, (q, q), 0)
    cols = lax.broadcasted_iota(I32, (q, q), 1)
    causal = rows >= cols
    tril = jnp.where(causal, 1.0, 0.0)
    triu = jnp.where(rows <= cols, 1.0, 0.0)
    a_cs = _dot_exact_lhs(tril, da)
    a_cst = _dot_exact_rhs(dat, triu)
    expand = e_ref[...]
    dt_e = _dot_exact_rhs(dt, expand)
    acs_e = _dot_exact_rhs(a_cs, expand)
    expa = jnp.exp(acs_e)
    a_last = acs_e[q - 1:q, :]
    xdt = xs * dt_e
    xdt_b = xdt.astype(BF16)
    xdec_b = (xdt * jnp.exp(a_last - acs_e)).astype(BF16)
    st = st_ref[...]
    st_b = st.astype(BF16)
    lane = lax.broadcasted_iota(I32, (q, 2 * SSM_HEAD_DIM), 1)

    for g in range(SSM_GROUPS):
        bg = bm[:, g * n:(g + 1) * n]
        cg = cm[:, g * n:(g + 1) * n].astype(BF16)
        cb = _dot_nt(cg, bg.astype(BF16))
        gs = slice(g * gw, (g + 1) * gw)
        y_off = _dot(cg, st_b[:, gs])
        s_new = _dot(bg.T.astype(BF16), xdec_b[:, gs])
        st_ref[:, gs] = st[:, gs] * expa[q - 1:q, gs] + s_new
        for j in range(hpg // 2):
            h0 = g * hpg + 2 * j
            c0 = h0 * SSM_HEAD_DIM
            xp = xdt_b[:, c0:c0 + 2 * SSM_HEAD_DIM]
            parts = []
            for h in (h0, h0 + 1):
                seg = a_cs[:, h:h + 1] - a_cst[h:h + 1, :]
                lmat = jnp.exp(jnp.where(causal, seg, -jnp.inf))
                parts.append(_dot((cb * lmat).astype(BF16), xp))
            y_diag = jnp.where(lane < SSM_HEAD_DIM, parts[0], parts[1])
            cs = slice(c0, c0 + 2 * SSM_HEAD_DIM)
            y_ref[:, cs] = y_diag + y_off[:, c0 - g * gw:c0 - g * gw + 2 * SSM_HEAD_DIM] * expa[:, cs]

    y = y_ref[...] + dskip_ref[...] * xs
    zz = z_ref[...].astype(F32)
    yg = y * (zz * _sigmoid(zz))
    out = yg * lax.rsqrt(jnp.mean(yg * yg, axis=-1, keepdims=True) + EPS) * norm_ref[...]
    o_ref[...] = out.astype(o_ref.dtype)


def _ssd(z, xbc, small, dtt, conv_w, conv_b, dt_bias, a_log, d_skip, ssm_norm, batch, seq):
    heads = dt_bias.shape[0]
    d_inner = heads * SSM_HEAD_DIM
    conv_dim = xbc.shape[1]
    q = SSM_CHUNK
    nc = seq // q
    expand = jnp.repeat(jnp.eye(LANES, heads, dtype=BF16), SSM_HEAD_DIM, axis=1)
    lane_pad = lambda v: jnp.pad(v.reshape(1, -1), ((0, 0), (0, LANES - heads)))
    row = lambda b, c: (b * nc + c, 0)
    const = lambda b, c: (0, 0)
    kern = functools.partial(_ssd_kernel, heads=heads, d_inner=d_inner)
    return pl.pallas_call(
        kern,
        grid=(batch, nc),
        in_specs=[
            pl.BlockSpec((q, d_inner), row),
            pl.BlockSpec((q, conv_dim), row),
            pl.BlockSpec((q, LANES), row),
            pl.BlockSpec((None, heads, q), lambda b, c: (b, 0, c)),
            pl.BlockSpec(conv_w.T.shape, const),
            pl.BlockSpec((1, conv_dim), const),
            pl.BlockSpec((1, LANES), const),
            pl.BlockSpec((heads, 1), const),
            pl.BlockSpec((1, LANES), const),
            pl.BlockSpec((heads, 1), const),
            pl.BlockSpec((1, d_inner), const),
            pl.BlockSpec((1, d_inner), const),
            pl.BlockSpec((LANES, d_inner), const),
        ],
        out_specs=pl.BlockSpec((q, d_inner), row),
        out_shape=jax.ShapeDtypeStruct((batch * seq, d_inner), BF16),
        scratch_shapes=[pltpu.VMEM((16, conv_dim), xbc.dtype),
                        pltpu.VMEM((SSM_STATE, d_inner), F32),
                        pltpu.VMEM((q, d_inner), F32)],
        compiler_params=_cparams("parallel", "arbitrary"),
    )(z, xbc, small, dtt, conv_w.T, conv_b.reshape(1, -1),
      lane_pad(dt_bias), dt_bias.reshape(-1, 1), lane_pad(a_log), a_log.reshape(-1, 1),
      jnp.repeat(d_skip, SSM_HEAD_DIM).reshape(1, -1), ssm_norm.reshape(1, -1), expand)


def _order_key(x):
    bits = int(np.array(x, np.float32).view(np.int32))
    return bits ^ ((bits >> 31) & 0x7FFFFFFF)


def _dsa_consts(seq, att_heads):
    hd = ATT_HEAD_DIM
    assert seq <= 16 * 256 and 2 * hd == LANES
    qc = np.zeros((att_heads, LANES), np.float32)
    for h in range(att_heads):
        rest = np.float32(2.0 ** (-8.0 * (h + 1) / att_heads) * LOG2E)
        for i in range(3):
            piece = np.float32(rest.astype(jnp.bfloat16))
            rest = np.float32(rest - piece)
            qc[h, hd + i] = 16.0 * piece
            qc[h, hd + 3 + i] = piece
    pos = np.arange(seq)
    kpos = np.zeros((seq, LANES), np.float32)
    kpos[:, hd:hd + 3] = (pos // 16)[:, None]
    kpos[:, hd + 3:hd + 6] = (pos % 16)[:, None]
    return jnp.asarray(qc), jnp.asarray(kpos, dtype=BF16)


def _dsa_kernel(q_ref, qi_ref, ka_ref, kip_ref, vt_ref, wit_ref, qc_ref, o_ref,
                key_ref, mb_ref, qa_ref, *, topk, att_heads, idx_heads, widths):
    qb = Q_BLOCK
    seq = ka_ref.shape[0]
    nq = seq // qb
    blk = pl.program_id(1)
    hd = ATT_HEAD_DIM
    kf = float(topk)
    key_ni = _order_key(NEG_INF)
    left = lax.broadcasted_iota(I32, (qb, LANES), 1) < hd
    tpos = blk * qb + lax.broadcasted_iota(I32, (1, qb), 1)

    hpd = DSA_HEADS_PER_DOT
    for h in range(att_heads):
        qpair = q_ref[:, (h // 2) * LANES:(h // 2 + 1) * LANES].astype(F32)
        qh = qpair if h % 2 == 0 else pltpu.roll(qpair, hd, axis=1)
        qa_ref[h // hpd, (h % hpd) * qb:(h % hpd + 1) * qb, :] = jnp.where(
            left, qh, qc_ref[h:h + 1, :]).astype(BF16)

    def body(w):
        n_out = float(seq - w)
        spos = lax.broadcasted_iota(I32, (w, 1), 0)
        causal = spos <= tpos

        wit = wit_ref[...] * (idx_heads ** -0.5 * IDX_HEAD_DIM ** -0.5)
        isc = jnp.zeros((w, qb), F32)
        qi_rows = jnp.concatenate([qi_ref[:, g * LANES:(g + 1) * LANES]
                                   for g in range(idx_heads // 2)], axis=0)
        for par in range(2):
            rel = _dot_nt(kip_ref[:w, par * LANES:(par + 1) * LANES], qi_rows)
            for g in range(idx_heads // 2):
                h = 2 * g + par
                isc = isc + jnp.maximum(rel[:, g * qb:(g + 1) * qb], 0.0) * wit[h:h + 1, :]
        masked = jnp.where(causal, isc, NEG_INF) + 0.0
        bits = pltpu.bitcast(masked, I32)
        key_ref[:w, :] = bits ^ ((bits >> 31) & 0x7FFFFFFF)

        def count(mask):
            return _reduce_rows(jnp.where(mask, 1.0, 0.0), jnp.sum, jnp.add)

        def count_ge(cand):
            return count(key_ref[:w, :] >= cand) + jnp.where(key_ni >= cand, n_out, 0.0)

        int_min = jnp.full((1, qb), -2 ** 31, I32)
        thr0 = jnp.where(count_ge(jnp.zeros((1, qb), I32)) >= kf, 0, int_min)

        def thr_body(j, thr):
            cand = thr + lax.shift_left(jnp.int32(1), 30 - j)
            return jnp.where(count_ge(cand) >= kf, cand, thr)

        thr = lax.fori_loop(0, 31, thr_body, thr0)

        key = key_ref[:w, :]
        gt = key > thr
        eqc = (key == thr) & causal
        need = kf - count(gt) - jnp.where(key_ni > thr, n_out, 0.0)
        has_tie = jnp.max(jnp.where(count(eqc) > need, 1.0, 0.0)) > 0.0

        @pl.when(jnp.logical_not(has_tie))
        def _():
            mb_ref[:w, :] = jnp.where(causal & (key >= thr), 0.0, -jnp.inf)

        @pl.when(has_tie)
        def _():
            nbits = max(1, (w - 1).bit_length())

            def cut_body(b, cut):
                cand = cut + lax.shift_left(jnp.int32(1), nbits - 1 - b)
                eq_here = (key_ref[:w, :] == thr) & causal
                return jnp.where(count(eq_here & (spos < cand)) < need, cand, cut)

            cut = lax.fori_loop(0, nbits, cut_body, jnp.zeros((1, qb), I32))
            sel = gt | (eqc & (spos <= cut))
            mb_ref[:w, :] = jnp.where(causal & sel, 0.0, -jnp.inf)

        n_dots = att_heads // hpd
        s_next = _dot_nt(ka_ref[:w, :], qa_ref[0])
        for jd in range(n_dots):
            s = s_next
            if jd + 1 < n_dots:
                s_next = _dot_nt(ka_ref[:w, :], qa_ref[jd + 1])
            ps = []
            for e in range(hpd):
                sh = s[:, e * qb:(e + 1) * qb] + mb_ref[:w, :]
                ps.append(jnp.exp2((sh - _reduce_rows(sh, jnp.max, jnp.maximum)).astype(BF16)))
            acc = _dot(vt_ref[:, :w], jnp.concatenate(ps, axis=1))
            o = acc * (1.0 / acc[hd:hd + 1, :])
            for e in range(0, hpd, 2):
                pair = jnp.where(left, o[:, e * qb:(e + 1) * qb].T,
                                 pltpu.roll(o[:, (e + 1) * qb:(e + 2) * qb].T, hd, axis=1))
                g = (jd * hpd + e) // 2
                o_ref[:, g * LANES:(g + 1) * LANES] = pair.astype(o_ref.dtype)

    per = nq // widths
    for v in range(widths):
        @pl.when((blk >= v * per) & (blk < (v + 1) * per))
        def _(v=v):
            body((v + 1) * per * qb)


def _dsa(qq, kvp, wit, batch, seq, att_heads, idx_heads):
    qb = Q_BLOCK
    nb = seq // qb
    hd = ATT_HEAD_DIM
    att_w = att_heads * hd
    idx_w = idx_heads * IDX_HEAD_DIM
    widths = min(DSA_WIDTHS, nb)
    assert att_w % idx_w == 0 and nb % widths == 0 and idx_heads % 2 == 0
    assert att_heads % DSA_HEADS_PER_DOT == 0 and DSA_HEADS_PER_DOT % 2 == 0
    assert 2 * IDX_HEAD_DIM == LANES and qb == LANES
    topk = min(TOPK_MAX, seq // 4)
    qc, kpos = _dsa_consts(seq, att_heads)
    ka = kvp[:, 2 * LANES:3 * LANES] + jnp.tile(kpos, (batch, 1))
    vt = jnp.swapaxes(kvp[:, 3 * LANES:].reshape(batch, seq, LANES), 1, 2)
    vt = vt.at[:, hd, :].set(1.0)
    kern = functools.partial(_dsa_kernel, topk=topk, att_heads=att_heads, idx_heads=idx_heads,
                             widths=widths)
    return pl.pallas_call(
        kern,
        grid=(batch, nb),
        in_specs=[
            pl.BlockSpec((qb, att_w), lambda b, i: (b * nb + i, 0)),
            pl.BlockSpec((qb, idx_w), lambda b, i: (b * nb + i, att_w // idx_w)),
            pl.BlockSpec((seq, LANES), lambda b, i: (b, 0)),
            pl.BlockSpec((seq, 2 * LANES), lambda b, i: (b, 0)),
            pl.BlockSpec((None, LANES, seq), lambda b, i: (b, 0, 0)),
            pl.BlockSpec((None, idx_heads, qb), lambda b, i: (b, 0, i)),
            pl.BlockSpec(qc.shape, lambda b, i: (0, 0)),
        ],
        out_specs=pl.BlockSpec((qb, att_w), lambda b, i: (b * nb + i, 0)),
        out_shape=jax.ShapeDtypeStruct((batch * seq, att_w), BF16),
        scratch_shapes=[pltpu.VMEM((seq, qb), I32),
                        pltpu.VMEM((seq, qb), F32),
                        pltpu.VMEM((att_heads // DSA_HEADS_PER_DOT, DSA_HEADS_PER_DOT * qb, LANES),
                                   BF16)],
        compiler_params=_cparams("parallel", "parallel"),
    )(qq, qq, ka, kvp, vt, wit, qc)


def _merge_kernel(x_ref, ys_ref, oa_ref, g_ref, wso_ref, wao_ref, wmo_ref, nx_ref,
                  x1_ref, xn_ref):
    d = x_ref.shape[1]
    y_ssm = _dot(ys_ref[...], wso_ref[...])
    y_att = _dot(oa_ref[...], wao_ref[...])
    g = g_ref[...].astype(F32)
    mix = _sigmoid(g[:, :d]) * y_ssm + _sigmoid(g[:, d:]) * y_att
    x1 = x_ref[...] + _dot(mix.astype(BF16), wmo_ref[...])
    x1_ref[...] = x1
    xn = x1 * lax.rsqrt(jnp.mean(x1 * x1, axis=-1, keepdims=True) + EPS) * nx_ref[...]
    xn_ref[...] = xn.astype(xn_ref.dtype)


def _merge(x, ys, oa, gates, wso, wao, wmo, norm_x, tm):
    t, d = x.shape
    row = lambda i: (i, 0)
    const = lambda i: (0, 0)
    return pl.pallas_call(
        _merge_kernel,
        grid=(t // tm,),
        in_specs=[pl.BlockSpec((tm, d), row),
                  pl.BlockSpec((tm, ys.shape[1]), row),
                  pl.BlockSpec((tm, oa.shape[1]), row),
                  pl.BlockSpec((tm, 2 * d), row),
                  pl.BlockSpec(wso.shape, const),
                  pl.BlockSpec(wao.shape, const),
                  pl.BlockSpec(wmo.shape, const),
                  pl.BlockSpec((1, d), const)],
        out_specs=[pl.BlockSpec((tm, d), row), pl.BlockSpec((tm, d), row)],
        out_shape=[jax.ShapeDtypeStruct((t, d), F32), jax.ShapeDtypeStruct((t, d), BF16)],
        compiler_params=_cparams("parallel"),
    )(x, ys, oa, gates, wso, wao, wmo, norm_x.reshape(1, d))


def _xattn_kernel(x1_ref, xn_ref, kv_ref, wq_ref, wo_ref, x2_ref):
    d = x1_ref.shape[1]
    hd = d // X_HEADS
    qf = _dot(xn_ref[...], wq_ref[...]) * (hd ** -0.5)
    qb = qf.astype(BF16)
    outs = []
    for h in range(X_HEADS):
        kh = kv_ref[:, h * hd:(h + 1) * hd]
        vh = kv_ref[:, d + h * hd:d + (h + 1) * hd]
        s = _dot_nt(qb[:, h * hd:(h + 1) * hd], kh)
        m = jnp.max(s, axis=-1, keepdims=True)
        p = jnp.exp(s - m)
        l = jnp.sum(p, axis=-1, keepdims=True)
        outs.append((_dot(p.astype(BF16), vh) / l).astype(BF16))
    o = jnp.concatenate(outs, axis=-1)
    x2_ref[...] = x1_ref[...] + _dot(o, wo_ref[...])


def _xattn(x1, xn, kvm, wq, wo, batch, seq, mem_len, tm):
    t, d = x1.shape
    nt = seq // tm
    row = lambda b, i: (b * nt + i, 0)
    const = lambda b, i: (0, 0)
    return pl.pallas_call(
        _xattn_kernel,
        grid=(batch, nt),
        in_specs=[pl.BlockSpec((tm, d), row),
                  pl.BlockSpec((tm, d), row),
                  pl.BlockSpec((mem_len, 2 * d), lambda b, i: (b, 0)),
                  pl.BlockSpec(wq.shape, const),
                  pl.BlockSpec(wo.shape, const)],
        out_specs=pl.BlockSpec((tm, d), row),
        out_shape=jax.ShapeDtypeStruct((t, d), F32),
        compiler_params=_cparams("parallel", "parallel"),
    )(x1, xn, kvm, wq, wo)


def _router_kernel(x_ref, g_ref, wr_ref, br_ref, xm_ref, route_ref, *, groups, per_group):
    x = x_ref[...]
    xm = x * lax.rsqrt(jnp.mean(x * x, axis=-1, keepdims=True) + EPS) * g_ref[...]
    xm_ref[...] = xm
    x0, x1, _ = _split3(xm)
    w0, w1, _ = _split3(wr_ref[...])
    logits = _dot(x0, w0) + _dot(x0, w1) + _dot(x1, w0) + br_ref[...]
    lane = lax.broadcasted_iota(I32, logits.shape, 1)
    lanef = lane.astype(F32)
    far = float(LANES)
    ninf = -jnp.inf

    gl = jnp.where(lane < groups, logits, ninf)
    gmax = jnp.max(gl, axis=-1, keepdims=True)
    gsel = jnp.min(jnp.where(gl == gmax, lanef, far), axis=-1, keepdims=True)
    ge = jnp.exp(gl - gmax)
    pg = jnp.max(ge / jnp.sum(ge, axis=-1, keepdims=True), axis=-1, keepdims=True)

    lo = groups + per_group * gsel
    em = (lanef >= lo) & (lanef < lo + per_group)
    el = jnp.where(em, logits, ninf)
    ee = jnp.exp(el - jnp.max(el, axis=-1, keepdims=True))
    ep = jnp.where(em, ee / jnp.sum(ee, axis=-1, keepdims=True), -1.0)
    p1 = jnp.max(ep, axis=-1, keepdims=True)
    i1 = jnp.min(jnp.where(ep == p1, lanef, far), axis=-1, keepdims=True)
    ep2 = jnp.where(lanef == i1, -1.0, ep)
    p2 = jnp.max(ep2, axis=-1, keepdims=True)
    i2 = jnp.min(jnp.where(ep2 == p2, lanef, far), axis=-1, keepdims=True)
    w1 = pg * p1 / (p1 + p2)
    w2 = pg * p2 / (p1 + p2)
    route_ref[...] = jnp.where(lane == 0, i1 - groups,
                     jnp.where(lane == 1, i2 - groups,
                     jnp.where(lane == 2, w1,
                     jnp.where(lane == 3, w2, 0.0))))


def _router(x2, norm_moe, wr, br, groups, per_group, tm):
    t, d = x2.shape
    row = lambda i: (i, 0)
    const = lambda i: (0, 0)
    kern = functools.partial(_router_kernel, groups=groups, per_group=per_group)
    return pl.pallas_call(
        kern,
        grid=(t // tm,),
        in_specs=[pl.BlockSpec((tm, d), row), pl.BlockSpec((1, d), const),
                  pl.BlockSpec((d, LANES), const), pl.BlockSpec((1, LANES), const)],
        out_specs=[pl.BlockSpec((tm, d), row), pl.BlockSpec((tm, LANES), row)],
        out_shape=[jax.ShapeDtypeStruct((t, d), F32), jax.ShapeDtypeStruct((t, LANES), F32)],
        compiler_params=_cparams("parallel"),
    )(x2, norm_moe.reshape(1, d), wr, br)


def _expert_kernel(te_ref, tv_ref, gcur_ref, gnext_ref, sidx_ref, xm_ref, wg_ref, wu_ref, wd_ref,
                   y_ref, xbuf, ybuf, wgb, wub, wdb, gsem, ssem):
    i = pl.program_id(0)
    nt = pl.num_programs(0)
    slot = i % 2
    groups8 = xbuf.shape[1]
    rows = 8 * groups8
    d = xbuf.shape[3]
    nxt = jnp.minimum(i + 1, nt - 1)
    valid = tv_ref[i] > 0
    has_next = (i + 1 < nt) & (tv_ref[nxt] > 0)

    def gather_copy(src_row, s, g8, u):
        return pltpu.make_async_copy(xm_ref.at[pl.ds(src_row, 1), :],
                                     xbuf.at[s, g8, pl.ds(u, 1), :], gsem.at[s])

    def scatter_copy(s, g8, u, dst_row):
        return pltpu.make_async_copy(ybuf.at[s, g8, pl.ds(u, 1), :],
                                     y_ref.at[pl.ds(dst_row, 1), :], ssem.at[s])

    def per_row(fn):
        def g8_body(g8, carry):
            for u in range(8):
                fn(g8, u)
            return carry
        lax.fori_loop(0, groups8, g8_body, 0)

    def gather_start(idx_ref, s):
        per_row(lambda g8, u: gather_copy(idx_ref[0, g8 * 8 + u], s, g8, u).start(priority=u % 2))

    def gather_wait(s):
        pltpu.make_async_copy(xbuf.at[s], xbuf.at[s], gsem.at[s]).wait()

    def scatter_start(s):
        per_row(lambda g8, u: scatter_copy(s, g8, u, sidx_ref[0, g8 * 8 + u]).start(priority=u % 2))

    def scatter_wait(s):
        pltpu.make_async_copy(ybuf.at[s], ybuf.at[s], ssem.at[s]).wait()

    @pl.when(i == 0)
    def _():
        ybuf[1] = jnp.zeros(ybuf.shape[1:], ybuf.dtype)
        n_real = y_ref.shape[0] - 2 * rows

        def fill_copy(g):
            return pltpu.make_async_copy(ybuf.at[1, g % groups8],
                                         y_ref.at[pl.ds(n_real + g * 8, 8), :], ssem.at[1])

        def fill_start(g, carry):
            fill_copy(g).start()
            return carry

        def fill_wait(g, carry):
            fill_copy(g).wait()
            return carry

        lax.fori_loop(0, 2 * groups8, fill_start, 0)
        lax.fori_loop(0, 2 * groups8, fill_wait, 0)

    @pl.when((i == 0) & valid)
    def _():
        gather_start(gcur_ref, 0)

    @pl.when(valid & ((i == 0) | (te_ref[i] != te_ref[jnp.maximum(i - 1, 0)])))
    def _():
        wgb[...] = wg_ref[...].astype(BF16)
        wub[...] = wu_ref[...].astype(BF16)
        wdb[...] = wd_ref[...].astype(BF16)

    @pl.when(valid)
    def _():
        @pl.when(has_next)
        def _():
            gather_start(gnext_ref, 1 - slot)

        gather_wait(slot)

        @pl.when(i >= 2)
        def _():
            scatter_wait(slot)

        xb = xbuf[slot].reshape(rows, d).astype(BF16)
        gate = _dot(xb, wgb[...])
        hid = gate * _sigmoid(gate) * _dot(xb, wub[...])
        ybuf[slot] = _dot(hid.astype(BF16), wdb[...]).reshape(groups8, 8, d)
        scatter_start(slot)

        @pl.when(jnp.logical_not(has_next))
        def _():
            scatter_wait(slot)

            @pl.when(i >= 1)
            def _():
                scatter_wait(1 - slot)


def _experts(xm, tile_expert, tile_valid, gidx, sidx, wg, wu, wd, n_out):
    t, d = xm.shape
    tm = EXPERT_TILE
    nt = tile_expert.shape[0]
    hid = wg.shape[2]
    smem_rows = lambda imap: pl.BlockSpec((None, 1, tm), imap, memory_space=pltpu.SMEM)
    wspec = lambda shape: pl.BlockSpec((None,) + shape, lambda i, te, tv: (te[i], 0, 0))
    grid_spec = pltpu.PrefetchScalarGridSpec(
        num_scalar_prefetch=2,
        grid=(nt,),
        in_specs=[smem_rows(lambda i, te, tv: (i, 0, 0)),
                  smem_rows(lambda i, te, tv: (jnp.minimum(i + 1, nt - 1), 0, 0)),
                  smem_rows(lambda i, te, tv: (i, 0, 0)),
                  pl.BlockSpec(memory_space=pl.ANY),
                  wspec((d, hid)), wspec((d, hid)), wspec((hid, d))],
        out_specs=pl.BlockSpec(memory_space=pl.ANY),
        scratch_shapes=[pltpu.VMEM((2, tm // 8, 8, d), F32), pltpu.VMEM((2, tm // 8, 8, d), F32),
                        pltpu.VMEM((d, hid), BF16), pltpu.VMEM((d, hid), BF16),
                        pltpu.VMEM((hid, d), BF16),
                        pltpu.SemaphoreType.DMA((2,)), pltpu.SemaphoreType.DMA((2,))],
    )
    g3 = gidx.reshape(nt, 1, tm)
    return pl.pallas_call(
        _expert_kernel,
        grid_spec=grid_spec,
        out_shape=jax.ShapeDtypeStruct((n_out, d), F32),
        compiler_params=_cparams("arbitrary"),
    )(tile_expert, tile_valid, g3, g3, sidx.reshape(nt, 1, tm), xm, wg, wu, wd)


def _combine_kernel(x_ref, y1_ref, y2_ref, route_ref, g_ref, o_ref):
    w1 = route_ref[:, 2:3]
    w2 = route_ref[:, 3:4]
    x = x_ref[...] + w1 * y1_ref[...] + w2 * y2_ref[...]
    o_ref[...] = x * lax.rsqrt(jnp.mean(x * x, axis=-1, keepdims=True) + EPS) * g_ref[...]


def _combine(x2, y, route, norm_final, tm):
    t, d = x2.shape
    row = lambda i: (i, 0)
    return pl.pallas_call(
        _combine_kernel,
        grid=(t // tm,),
        in_specs=[pl.BlockSpec((tm, d), row), pl.BlockSpec((tm, d), row),
                  pl.BlockSpec((tm, d), lambda i: (i + t // tm, 0)),
                  pl.BlockSpec((tm, LANES), row), pl.BlockSpec((1, d), lambda i: (0, 0))],
        out_specs=pl.BlockSpec((tm, d), row),
        out_shape=jax.ShapeDtypeStruct((t, d), F32),
        compiler_params=_cparams("parallel"),
    )(x2, y, y, route, norm_final.reshape(1, d))


def _moe_plan(route, n_experts, tile):
    t = route.shape[0]
    pairs = MOE_TOPK * t
    n_tiles = pairs // tile + n_experts
    e_flat = route[:, :MOE_TOPK].astype(I32).reshape(pairs)
    order = jnp.argsort(e_flat, stable=True).astype(I32)
    counts = jnp.sum((e_flat[:, None] == jnp.arange(n_experts, dtype=I32)[None, :]).astype(I32), axis=0)
    tiles_per = (counts + tile - 1) // tile
    tile_end = jnp.cumsum(tiles_per)
    sorted_start = jnp.cumsum(counts) - counts
    tile_ids = jnp.arange(n_tiles, dtype=I32)
    tile_valid = (tile_ids < tile_end[-1]).astype(I32)
    tile_expert = jnp.minimum(jnp.sum((tile_ids[:, None] >= tile_end[None, :]).astype(I32), axis=1),
                              n_experts - 1)
    last_expert = tile_expert[jnp.maximum(tile_end[-1] - 1, 0)]
    tile_expert = jnp.where(tile_valid > 0, tile_expert, last_expert)
    tile_first = (tile_end - tiles_per)[tile_expert]
    lane_row = jnp.arange(tile, dtype=I32)[None, :]
    row_in_group = ((tile_ids - tile_first) * tile)[:, None] + lane_row
    row_valid = (row_in_group < counts[tile_expert][:, None]) & (tile_valid[:, None] > 0)
    src = jnp.clip(sorted_start[tile_expert][:, None] + row_in_group, 0, pairs - 1)
    pair = order[src]
    token = jnp.where(row_valid, pair // MOE_TOPK, 0)
    spare = pairs + (tile_ids % 2)[:, None] * tile + lane_row
    dest = jnp.where(row_valid, (pair % MOE_TOPK) * t + pair // MOE_TOPK, spare)
    return tile_expert, tile_valid, token.reshape(-1), dest.reshape(-1)


def kernel(x, mem, norm_mix, w_in, conv_w, conv_b, dt_bias, a_log, d_skip, ssm_norm, w_ssm_out,
           w_att_out, w_mix_out, norm_x, norm_mem, w_xq, w_xkv, w_xo, norm_moe, w_rg, b_rg, w_re,
           b_re, w_e_gate, w_e_up, w_e_down, norm_final):
    batch, seq, d = x.shape
    mem_len = mem.shape[1]
    t = batch * seq
    heads = dt_bias.shape[1]
    d_inner = heads * SSM_HEAD_DIM
    conv_dim = conv_w.shape[1]
    att_w = w_att_out.shape[1]
    att_heads = att_w // ATT_HEAD_DIM
    n_experts = w_re.shape[2]
    groups = w_rg.shape[2]
    per_group = n_experts // groups
    idx_heads = (w_in.shape[2] - (d_inner + conv_dim + heads + att_w + 2 * ATT_HEAD_DIM
                                  + IDX_HEAD_DIM + 2 * d)) // (IDX_HEAD_DIM + 1)
    idx_w = idx_heads * IDX_HEAD_DIM
    assert w_in.shape[0] == 1, "one layer; the final norm is fused into the layer's last call"
    assert seq % SSM_CHUNK == 0 and seq % Q_BLOCK == 0 and heads + idx_heads <= LANES
    assert groups + n_experts <= LANES and t % 1024 == 0 and heads % (2 * SSM_GROUPS) == 0
    li = 0

    sizes = (d_inner, conv_dim, heads, att_w, ATT_HEAD_DIM, ATT_HEAD_DIM, idx_w, IDX_HEAD_DIM,
             idx_heads, d, d)
    offs = [0]
    for s in sizes:
        offs.append(offs[-1] + s)
    wi_ = w_in[li]
    col = lambda k: wi_[:, offs[k]:offs[k + 1]]
    pad = lambda a, n: jnp.pad(a, ((0, 0), (0, n - a.shape[1])))
    lpad = lambda a, n: jnp.pad(a, ((0, 0), (n - a.shape[1], 0)))
    both = lambda a: [pad(a, LANES), lpad(a, LANES)]
    w_z = col(0).astype(BF16)
    w_xbc = col(1).astype(BF16)
    w_small = pad(jnp.concatenate([col(2), col(8)], axis=1), LANES).astype(BF16)
    w_qq = jnp.concatenate([col(3) * (LOG2E * ATT_HEAD_DIM ** -0.5), col(6)], axis=1).astype(BF16)
    w_kv = jnp.concatenate(both(col(7)) + [pad(col(4), LANES), pad(col(5), LANES)],
                           axis=1).astype(BF16)
    w_gates = jnp.concatenate([col(9), col(10)], axis=1).astype(BF16)

    h = x.reshape(t, d)
    u = _rmsnorm(h, norm_mix[li], BF16, 512)
    z = _matmul(u, w_z, BF16, 1024, 1024)
    xbc = _matmul(u, w_xbc, BF16, 1024, 1024)
    small = _matmul(u, w_small, F32, 512, LANES)
    qq = _matmul(u, w_qq, BF16, 512, att_w + idx_w)
    kvp = _matmul(u, w_kv, BF16, 512, 4 * LANES)
    gates = _matmul(u, w_gates, BF16, 1024, 1024)

    dtt = jnp.swapaxes(small[:, :heads].reshape(batch, seq, heads), 1, 2)
    ys = _ssd(z, xbc, small, dtt, conv_w[li], conv_b[li], dt_bias[li], a_log[li], d_skip[li],
              ssm_norm[li], batch, seq)
    wit = jnp.swapaxes(small[:, heads:heads + idx_heads].reshape(batch, seq, idx_heads), 1, 2)
    oa = _dsa(qq, kvp, wit, batch, seq, att_heads, idx_heads)

    x1, xn1 = _merge(h, ys, oa, gates, w_ssm_out[li].astype(BF16), w_att_out[li].astype(BF16),
                     w_mix_out[li].astype(BF16), norm_x[li], 512)

    mn = _rmsnorm(mem.reshape(batch * mem_len, d), norm_mem[li], BF16, 512)
    kvm = _matmul(mn, w_xkv[li].astype(BF16), BF16, 512, 1024)
    x2 = _xattn(x1, xn1, kvm, w_xq[li].astype(BF16), w_xo[li].astype(BF16), batch, seq,
                mem_len, 512)

    wr = pad(jnp.concatenate([w_rg[li], w_re[li]], axis=1), LANES)
    br = pad(jnp.concatenate([b_rg[li], b_re[li]]).reshape(1, -1), LANES)
    xm, route = _router(x2, norm_moe[li], wr, br, groups, per_group, 512)

    tile_expert, tile_valid, token, dest = _moe_plan(route, n_experts, EXPERT_TILE)
    ye = _experts(xm, tile_expert, tile_valid, token, dest, w_e_gate[li], w_e_up[li], w_e_down[li],
                  MOE_TOPK * t + 2 * EXPERT_TILE)
    out = _combine(x2, ye, route, norm_final, 512)
    return out.reshape(batch, seq, d)
```

```python
import functools
import math

import numpy as np
import jax
import jax.numpy as jnp
from jax import lax
from jax.experimental import pallas as pl
from jax.experimental.pallas import tpu as pltpu

F32 = jnp.float32
BF16 = jnp.bfloat16
I32 = jnp.int32
EPS = 1e-6
LOG2E = math.log2(math.e)

SSM_HEAD_DIM = 64
SSM_GROUPS = 4
SSM_STATE = 128
SSM_CHUNK = 128
ATT_HEAD_DIM = 64
IDX_HEAD_DIM = 64
TOPK_MAX = 256
Q_BLOCK = 128
NEG_INF = -1e30
X_HEADS = 4
MOE_GROUPS = 4
MOE_TOPK = 2

LANES = 128
VMEM_LIMIT = 56 * 1024 * 1024
EXPERT_TILE = 256
DSA_WIDTHS = 4
DSA_HEADS_PER_DOT = 4


def _cparams(*sem):
    return pltpu.CompilerParams(dimension_semantics=sem, vmem_limit_bytes=VMEM_LIMIT)


def _sigmoid(x):
    return 1.0 / (1.0 + jnp.exp(-x))


def _softplus(x):
    return jnp.maximum(x, 0.0) + jnp.log1p(jnp.exp(-jnp.abs(x)))


def _dot(a, b):
    return jnp.dot(a, b, preferred_element_type=F32)


def _split3(x):
    p0 = x.astype(BF16)
    r1 = x - p0.astype(F32)
    p1 = r1.astype(BF16)
    p2 = (r1 - p1.astype(F32)).astype(BF16)
    return p0, p1, p2


def _dot_exact_rhs(x, m):
    mb = m.astype(BF16)
    p0, p1, p2 = _split3(x)
    return _dot(p0, mb) + _dot(p1, mb) + _dot(p2, mb)


def _dot_exact_lhs(m, x):
    mb = m.astype(BF16)
    p0, p1, p2 = _split3(x)
    return _dot(mb, p0) + _dot(mb, p1) + _dot(mb, p2)


def _dot_nt(a, b):
    return lax.dot_general(a, b, (((1,), (1,)), ((), ())), preferred_element_type=F32)


def _reduce_rows(x, op, pair_op, chains=8):
    w, n = x.shape
    while w % (8 * chains):
        chains //= 2
    step = w // chains
    parts = [op(x[i * step:(i + 1) * step].reshape(step // 8, 8, n), axis=0) for i in range(chains)]
    while len(parts) > 1:
        parts = [pair_op(parts[i], parts[i + 1]) for i in range(0, len(parts), 2)]
    return op(parts[0], axis=0, keepdims=True)


def _rmsnorm_kernel(x_ref, g_ref, o_ref):
    x = x_ref[...]
    y = x * lax.rsqrt(jnp.mean(x * x, axis=-1, keepdims=True) + EPS) * g_ref[...]
    o_ref[...] = y.astype(o_ref.dtype)


def _rmsnorm(x, g, out_dtype, tm):
    m, d = x.shape
    return pl.pallas_call(
        _rmsnorm_kernel,
        grid=(m // tm,),
        in_specs=[pl.BlockSpec((tm, d), lambda i: (i, 0)),
                  pl.BlockSpec((1, d), lambda i: (0, 0))],
        out_specs=pl.BlockSpec((tm, d), lambda i: (i, 0)),
        out_shape=jax.ShapeDtypeStruct((m, d), out_dtype),
        compiler_params=_cparams("parallel"),
    )(x, g.reshape(1, d))


def _mm_kernel(a_ref, b_ref, o_ref):
    o_ref[...] = _dot(a_ref[...], b_ref[...]).astype(o_ref.dtype)


def _matmul(a, b, out_dtype, tm, tn):
    m, k = a.shape
    n = b.shape[1]
    return pl.pallas_call(
        _mm_kernel,
        grid=(n // tn, m // tm),
        in_specs=[pl.BlockSpec((tm, k), lambda j, i: (i, 0)),
                  pl.BlockSpec((k, tn), lambda j, i: (0, j))],
        out_specs=pl.BlockSpec((tm, tn), lambda j, i: (i, j)),
        out_shape=jax.ShapeDtypeStruct((m, n), out_dtype),
        compiler_params=_cparams("parallel", "parallel"),
    )(a, b)


def _ssd_kernel(z_ref, xbc_ref, dt_ref, dtt_ref, cw_ref, cb_ref, dtb_ref, dtbt_ref,
                alog_ref, alogt_ref, dskip_ref, norm_ref, e_ref, o_ref,
                tail_ref, st_ref, y_ref, *, heads, d_inner):
    q = SSM_CHUNK
    n = SSM_STATE
    hpg = heads // SSM_GROUPS
    gw = hpg * SSM_HEAD_DIM
    c = pl.program_id(1)

    @pl.when(c == 0)
    def _():
        tail_ref[...] = jnp.zeros_like(tail_ref)
        st_ref[...] = jnp.zeros_like(st_ref)

    x = xbc_ref[...]
    tl = tail_ref.shape[0]
    xcat = jnp.concatenate([tail_ref[...], x], axis=0)
    tail_ref[...] = x[q - tl:, :]
    cw = cw_ref[...]
    kconv = cw.shape[0]
    srow = lax.broadcasted_iota(I32, ((kconv - 1) * q, tl + q), 0)
    scol = lax.broadcasted_iota(I32, ((kconv - 1) * q, tl + q), 1)
    shift = jnp.where(scol - (tl - (kconv - 1)) == srow - (q - 1) * (srow // q), 1.0, 0.0)
    shifted = _dot(shift.astype(xcat.dtype), xcat)
    acc = cb_ref[...] + cw[kconv - 1:kconv, :] * x.astype(F32)
    for k in range(kconv - 1):
        acc = acc + cw[k:k + 1, :] * shifted[k * q:(k + 1) * q, :]
    xc = acc * _sigmoid(acc)
    xs = xc[:, :d_inner]
    bm = xc[:, d_inner:d_inner + SSM_GROUPS * n]
    cm = xc[:, d_inner + SSM_GROUPS * n:]

    dt = _softplus(dt_ref[...] + dtb_ref[...])
    da = dt * (-jnp.exp(alog_ref[...]))
    dtt = _softplus(dtt_ref[...] + dtbt_ref[...])
    dat = dtt * (-jnp.exp(alogt_ref[...]))
    rows = lax.broadcasted_iota(I32, (q, q), 0)
    cols = lax.broadcasted_iota(I32, (q, q), 1)
    causal = rows >= cols
    tril = jnp.where(causal, 1.0, 0.0)
    triu = jnp.where(rows <= cols, 1.0, 0.0)
    a_cs = _dot_exact_lhs(tril, da)
    a_cst = _dot_exact_rhs(dat, triu)
    expand = e_ref[...]
    dt_e = _dot_exact_rhs(dt, expand)
    acs_e = _dot_exact_rhs(a_cs, expand)
    expa = jnp.exp(acs_e)
    a_last = acs_e[q - 1:q, :]
    xdt = xs * dt_e
    xdt_b = xdt.astype(BF16)
    xdec_b = (xdt * jnp.exp(a_last - acs_e)).astype(BF16)
    st = st_ref[...]
    st_b = st.astype(BF16)
    lane = lax.broadcasted_iota(I32, (q, 2 * SSM_HEAD_DIM), 1)

    for g in range(SSM_GROUPS):
        bg = bm[:, g * n:(g + 1) * n]
        cg = cm[:, g * n:(g + 1) * n].astype(BF16)
        cb = _dot_nt(cg, bg.astype(BF16))
        gs = slice(g * gw, (g + 1) * gw)
        y_off = _dot(cg, st_b[:, gs])
        s_new = _dot(bg.T.astype(BF16), xdec_b[:, gs])
        st_ref[:, gs] = st[:, gs] * expa[q - 1:q, gs] + s_new
        for j in range(hpg // 2):
            h0 = g * hpg + 2 * j
            c0 = h0 * SSM_HEAD_DIM
            xp = xdt_b[:, c0:c0 + 2 * SSM_HEAD_DIM]
            parts = []
            for h in (h0, h0 + 1):
                seg = a_cs[:, h:h + 1] - a_cst[h:h + 1, :]
                lmat = jnp.exp(jnp.where(causal, seg, -jnp.inf))
                parts.append(_dot((cb * lmat).astype(BF16), xp))
            y_diag = jnp.where(lane < SSM_HEAD_DIM, parts[0], parts[1])
            cs = slice(c0, c0 + 2 * SSM_HEAD_DIM)
            y_ref[:, cs] = y_diag + y_off[:, c0 - g * gw:c0 - g * gw + 2 * SSM_HEAD_DIM] * expa[:, cs]

    y = y_ref[...] + dskip_ref[...] * xs
    zz = z_ref[...].astype(F32)
    yg = y * (zz * _sigmoid(zz))
    out = yg * lax.rsqrt(jnp.mean(yg * yg, axis=-1, keepdims=True) + EPS) * norm_ref[...]
    o_ref[...] = out.astype(o_ref.dtype)


def _ssd(z, xbc, small, dtt, conv_w, conv_b, dt_bias, a_log, d_skip, ssm_norm, batch, seq):
    heads = dt_bias.shape[0]
    d_inner = heads * SSM_HEAD_DIM
    conv_dim = xbc.shape[1]
    q = SSM_CHUNK
    nc = seq // q
    expand = jnp.repeat(jnp.eye(LANES, heads, dtype=BF16), SSM_HEAD_DIM, axis=1)
    lane_pad = lambda v: jnp.pad(v.reshape(1, -1), ((0, 0), (0, LANES - heads)))
    row = lambda b, c: (b * nc + c, 0)
    const = lambda b, c: (0, 0)
    kern = functools.partial(_ssd_kernel, heads=heads, d_inner=d_inner)
    return pl.pallas_call(
        kern,
        grid=(batch, nc),
        in_specs=[
            pl.BlockSpec((q, d_inner), row),
            pl.BlockSpec((q, conv_dim), row),
            pl.BlockSpec((q, LANES), row),
            pl.BlockSpec((None, heads, q), lambda b, c: (b, 0, c)),
            pl.BlockSpec(conv_w.T.shape, const),
            pl.BlockSpec((1, conv_dim), const),
            pl.BlockSpec((1, LANES), const),
            pl.BlockSpec((heads, 1), const),
            pl.BlockSpec((1, LANES), const),
            pl.BlockSpec((heads, 1), const),
            pl.BlockSpec((1, d_inner), const),
            pl.BlockSpec((1, d_inner), const),
            pl.BlockSpec((LANES, d_inner), const),
        ],
        out_specs=pl.BlockSpec((q, d_inner), row),
        out_shape=jax.ShapeDtypeStruct((batch * seq, d_inner), BF16),
        scratch_shapes=[pltpu.VMEM((16, conv_dim), xbc.dtype),
                        pltpu.VMEM((SSM_STATE, d_inner), F32),
                        pltpu.VMEM((q, d_inner), F32)],
        compiler_params=_cparams("parallel", "arbitrary"),
    )(z, xbc, small, dtt, conv_w.T, conv_b.reshape(1, -1),
      lane_pad(dt_bias), dt_bias.reshape(-1, 1), lane_pad(a_log), a_log.reshape(-1, 1),
      jnp.repeat(d_skip, SSM_HEAD_DIM).reshape(1, -1), ssm_norm.reshape(1, -1), expand)


def _order_key(x):
    bits = int(np.array(x, np.float32).view(np.int32))
    return bits ^ ((bits >> 31) & 0x7FFFFFFF)


def _dsa_consts(seq, att_heads):
    hd = ATT_HEAD_DIM
    assert seq <= 16 * 256 and 2 * hd == LANES
    qc = np.zeros((att_heads, LANES), np.float32)
    for h in range(att_heads):
        rest = np.float32(2.0 ** (-8.0 * (h + 1) / att_heads) * LOG2E)
        for i in range(3):
            piece = np.float32(rest.astype(jnp.bfloat16))
            rest = np.float32(rest - piece)
            qc[h, hd + i] = 16.0 * piece
            qc[h, hd + 3 + i] = piece
    pos = np.arange(seq)
    kpos = np.zeros((seq, LANES), np.float32)
    kpos[:, hd:hd + 3] = (pos // 16)[:, None]
    kpos[:, hd + 3:hd + 6] = (pos % 16)[:, None]
    return jnp.asarray(qc), jnp.asarray(kpos, dtype=BF16)


def _dsa_kernel(q_ref, qi_ref, ka_ref, kip_ref, vt_ref, wit_ref, qc_ref, o_ref,
                key_ref, mb_ref, qa_ref, *, topk, att_heads, idx_heads, widths):
    qb = Q_BLOCK
    seq = ka_ref.shape[0]
    nq = seq // qb
    blk = pl.program_id(1)
    hd = ATT_HEAD_DIM
    kf = float(topk)
    key_ni = _order_key(NEG_INF)
    left = lax.broadcasted_iota(I32, (qb, LANES), 1) < hd
    tpos = blk * qb + lax.broadcasted_iota(I32, (1, qb), 1)

    hpd = DSA_HEADS_PER_DOT
    for h in range(att_heads):
        qpair = q_ref[:, (h // 2) * LANES:(h // 2 + 1) * LANES].astype(F32)
        qh = qpair if h % 2 == 0 else pltpu.roll(qpair, hd, axis=1)
        qa_ref[h // hpd, (h % hpd) * qb:(h % hpd + 1) * qb, :] = jnp.where(
            left, qh, qc_ref[h:h + 1, :]).astype(BF16)

    def body(w):
        n_out = float(seq - w)
        spos = lax.broadcasted_iota(I32, (w, 1), 0)
        causal = spos <= tpos

        wit = wit_ref[...] * (idx_heads ** -0.5 * IDX_HEAD_DIM ** -0.5)
        isc = jnp.zeros((w, qb), F32)
        qi_rows = jnp.concatenate([qi_ref[:, g * LANES:(g + 1) * LANES]
                                   for g in range(idx_heads // 2)], axis=0)
        for par in range(2):
            rel = _dot_nt(kip_ref[:w, par * LANES:(par + 1) * LANES], qi_rows)
            for g in range(idx_heads // 2):
                h = 2 * g + par
                isc = isc + jnp.maximum(rel[:, g * qb:(g + 1) * qb], 0.0) * wit[h:h + 1, :]
        masked = jnp.where(causal, isc, NEG_INF) + 0.0
        bits = pltpu.bitcast(masked, I32)
        key_ref[:w, :] = bits ^ ((bits >> 31) & 0x7FFFFFFF)

        def count(mask):
            return _reduce_rows(jnp.where(mask, 1.0, 0.0), jnp.sum, jnp.add)

        def count_ge(cand):
            return count(key_ref[:w, :] >= cand) + jnp.where(key_ni >= cand, n_out, 0.0)

        int_min = jnp.full((1, qb), -2 ** 31, I32)
        thr0 = jnp.where(count_ge(jnp.zeros((1, qb), I32)) >= kf, 0, int_min)

        def thr_body(j, thr):
            cand = thr + lax.shift_left(jnp.int32(1), 30 - j)
            return jnp.where(count_ge(cand) >= kf, cand, thr)

        thr = lax.fori_loop(0, 31, thr_body, thr0)

        key = key_ref[:w, :]
        gt = key > thr
        eqc = (key == thr) & causal
        need = kf - count(gt) - jnp.where(key_ni > thr, n_out, 0.0)
        has_tie = jnp.max(jnp.where(count(eqc) > need, 1.0, 0.0)) > 0.0

        @pl.when(jnp.logical_not(has_tie))
        def _():
            mb_ref[:w, :] = jnp.where(causal & (key >= thr), 0.0, -jnp.inf)

        @pl.when(has_tie)
        def _():
            nbits = max(1, (w - 1).bit_length())

            def cut_body(b, cut):
                cand = cut + lax.shift_left(jnp.int32(1), nbits - 1 - b)
                eq_here = (key_ref[:w, :] == thr) & causal
                return jnp.where(count(eq_here & (spos < cand)) < need, cand, cut)

            cut = lax.fori_loop(0, nbits, cut_body, jnp.zeros((1, qb), I32))
            sel = gt | (eqc & (spos <= cut))
            mb_ref[:w, :] = jnp.where(causal & sel, 0.0, -jnp.inf)

        n_dots = att_heads // hpd
        s_next = _dot_nt(ka_ref[:w, :], qa_ref[0])
        for jd in range(n_dots):
            s = s_next
            if jd + 1 < n_dots:
                s_next = _dot_nt(ka_ref[:w, :], qa_ref[jd + 1])
            ps = []
            for e in range(hpd):
                sh = s[:, e * qb:(e + 1) * qb] + mb_ref[:w, :]
                ps.append(jnp.exp2((sh - _reduce_rows(sh, jnp.max, jnp.maximum)).astype(BF16)))
            acc = _dot(vt_ref[:, :w], jnp.concatenate(ps, axis=1))
            o = acc * (1.0 / acc[hd:hd + 1, :])
            for e in range(0, hpd, 2):
                pair = jnp.where(left, o[:, e * qb:(e + 1) * qb].T,
                                 pltpu.roll(o[:, (e + 1) * qb:(e + 2) * qb].T, hd, axis=1))
                g = (jd * hpd + e) // 2
                o_ref[:, g * LANES:(g + 1) * LANES] = pair.astype(o_ref.dtype)

    per = nq // widths
    for v in range(widths):
        @pl.when((blk >= v * per) & (blk < (v + 1) * per))
        def _(v=v):
            body((v + 1) * per * qb)


def _dsa(qq, kvp, wit, batch, seq, att_heads, idx_heads):
    qb = Q_BLOCK
    nb = seq // qb
    hd = ATT_HEAD_DIM
    att_w = att_heads * hd
    idx_w = idx_heads * IDX_HEAD_DIM
    widths = min(DSA_WIDTHS, nb)
    assert att_w % idx_w == 0 and nb % widths == 0 and idx_heads % 2 == 0
    assert att_heads % DSA_HEADS_PER_DOT == 0 and DSA_HEADS_PER_DOT % 2 == 0
    assert 2 * IDX_HEAD_DIM == LANES and qb == LANES
    topk = min(TOPK_MAX, seq // 4)
    qc, kpos = _dsa_consts(seq, att_heads)
    ka = kvp[:, 2 * LANES:3 * LANES] + jnp.tile(kpos, (batch, 1))
    vt = jnp.swapaxes(kvp[:, 3 * LANES:].reshape(batch, seq, LANES), 1, 2)
    vt = vt.at[:, hd, :].set(1.0)
    kern = functools.partial(_dsa_kernel, topk=topk, att_heads=att_heads, idx_heads=idx_heads,
                             widths=widths)
    return pl.pallas_call(
        kern,
        grid=(batch, nb),
        in_specs=[
            pl.BlockSpec((qb, att_w), lambda b, i: (b * nb + i, 0)),
            pl.BlockSpec((qb, idx_w), lambda b, i: (b * nb + i, att_w // idx_w)),
            pl.BlockSpec((seq, LANES), lambda b, i: (b, 0)),
            pl.BlockSpec((seq, 2 * LANES), lambda b, i: (b, 0)),
            pl.BlockSpec((None, LANES, seq), lambda b, i: (b, 0, 0)),
            pl.BlockSpec((None, idx_heads, qb), lambda b, i: (b, 0, i)),
            pl.BlockSpec(qc.shape, lambda b, i: (0, 0)),
        ],
        out_specs=pl.BlockSpec((qb, att_w), lambda b, i: (b * nb + i, 0)),
        out_shape=jax.ShapeDtypeStruct((batch * seq, att_w), BF16),
        scratch_shapes=[pltpu.VMEM((seq, qb), I32),
                        pltpu.VMEM((seq, qb), F32),
                        pltpu.VMEM((att_heads // DSA_HEADS_PER_DOT, DSA_HEADS_PER_DOT * qb, LANES),
                                   BF16)],
        compiler_params=_cparams("parallel", "parallel"),
    )(qq, qq, ka, kvp, vt, wit, qc)


def _merge_kernel(x_ref, ys_ref, oa_ref, g_ref, wso_ref, wao_ref, wmo_ref, nx_ref,
                  x1_ref, xn_ref):
    d = x_ref.shape[1]
    y_ssm = _dot(ys_ref[...], wso_ref[...])
    y_att = _dot(oa_ref[...], wao_ref[...])
    g = g_ref[...].astype(F32)
    mix = _sigmoid(g[:, :d]) * y_ssm + _sigmoid(g[:, d:]) * y_att
    x1 = x_ref[...] + _dot(mix.astype(BF16), wmo_ref[...])
    x1_ref[...] = x1
    xn = x1 * lax.rsqrt(jnp.mean(x1 * x1, axis=-1, keepdims=True) + EPS) * nx_ref[...]
    xn_ref[...] = xn.astype(xn_ref.dtype)


def _merge(x, ys, oa, gates, wso, wao, wmo, norm_x, tm):
    t, d = x.shape
    row = lambda i: (i, 0)
    const = lambda i: (0, 0)
    return pl.pallas_call(
        _merge_kernel,
        grid=(t // tm,),
        in_specs=[pl.BlockSpec((tm, d), row),
                  pl.BlockSpec((tm, ys.shape[1]), row),
                  pl.BlockSpec((tm, oa.shape[1]), row),
                  pl.BlockSpec((tm, 2 * d), row),
                  pl.BlockSpec(wso.shape, const),
                  pl.BlockSpec(wao.shape, const),
                  pl.BlockSpec(wmo.shape, const),
                  pl.BlockSpec((1, d), const)],
        out_specs=[pl.BlockSpec((tm, d), row), pl.BlockSpec((tm, d), row)],
        out_shape=[jax.ShapeDtypeStruct((t, d), F32), jax.ShapeDtypeStruct((t, d), BF16)],
        compiler_params=_cparams("parallel"),
    )(x, ys, oa, gates, wso, wao, wmo, norm_x.reshape(1, d))


def _xattn_kernel(x1_ref, xn_ref, kv_ref, wq_ref, wo_ref, x2_ref):
    d = x1_ref.shape[1]
    hd = d // X_HEADS
    qf = _dot(xn_ref[...], wq_ref[...]) * (hd ** -0.5)
    qb = qf.astype(BF16)
    outs = []
    for h in range(X_HEADS):
        kh = kv_ref[:, h * hd:(h + 1) * hd]
        vh = kv_ref[:, d + h * hd:d + (h + 1) * hd]
        s = _dot_nt(qb[:, h * hd:(h + 1) * hd], kh)
        m = jnp.max(s, axis=-1, keepdims=True)
        p = jnp.exp(s - m)
        l = jnp.sum(p, axis=-1, keepdims=True)
        outs.append((_dot(p.astype(BF16), vh) / l).astype(BF16))
    o = jnp.concatenate(outs, axis=-1)
    x2_ref[...] = x1_ref[...] + _dot(o, wo_ref[...])


def _xattn(x1, xn, kvm, wq, wo, batch, seq, mem_len, tm):
    t, d = x1.shape
    nt = seq // tm
    row = lambda b, i: (b * nt + i, 0)
    const = lambda b, i: (0, 0)
    return pl.pallas_call(
        _xattn_kernel,
        grid=(batch, nt),
        in_specs=[pl.BlockSpec((tm, d), row),
                  pl.BlockSpec((tm, d), row),
                  pl.BlockSpec((mem_len, 2 * d), lambda b, i: (b, 0)),
                  pl.BlockSpec(wq.shape, const),
                  pl.BlockSpec(wo.shape, const)],
        out_specs=pl.BlockSpec((tm, d), row),
        out_shape=jax.ShapeDtypeStruct((t, d), F32),
        compiler_params=_cparams("parallel", "parallel"),
    )(x1, xn, kvm, wq, wo)


def _router_kernel(x_ref, g_ref, wr_ref, br_ref, route_ref, *, groups, per_group):
    x = x_ref[...]
    xm = x * lax.rsqrt(jnp.mean(x * x, axis=-1, keepdims=True) + EPS) * g_ref[...]
    x0, x1, _ = _split3(xm)
    w0, w1, _ = _split3(wr_ref[...])
    logits = _dot(x0, w0) + _dot(x0, w1) + _dot(x1, w0) + br_ref[...]
    lane = lax.broadcasted_iota(I32, logits.shape, 1)
    lanef = lane.astype(F32)
    far = float(LANES)
    ninf = -jnp.inf

    gl = jnp.where(lane < groups, logits, ninf)
    gmax = jnp.max(gl, axis=-1, keepdims=True)
    gsel = jnp.min(jnp.where(gl == gmax, lanef, far), axis=-1, keepdims=True)
    ge = jnp.exp(gl - gmax)
    pg = jnp.max(ge / jnp.sum(ge, axis=-1, keepdims=True), axis=-1, keepdims=True)

    lo = groups + per_group * gsel
    em = (lanef >= lo) & (lanef < lo + per_group)
    el = jnp.where(em, logits, ninf)
    ee = jnp.exp(el - jnp.max(el, axis=-1, keepdims=True))
    ep = jnp.where(em, ee / jnp.sum(ee, axis=-1, keepdims=True), -1.0)
    p1 = jnp.max(ep, axis=-1, keepdims=True)
    i1 = jnp.min(jnp.where(ep == p1, lanef, far), axis=-1, keepdims=True)
    ep2 = jnp.where(lanef == i1, -1.0, ep)
    p2 = jnp.max(ep2, axis=-1, keepdims=True)
    i2 = jnp.min(jnp.where(ep2 == p2, lanef, far), axis=-1, keepdims=True)
    w1 = pg * p1 / (p1 + p2)
    w2 = pg * p2 / (p1 + p2)
    route_ref[...] = jnp.where(lane == 0, i1 - groups,
                     jnp.where(lane == 1, i2 - groups,
                     jnp.where(lane == 2, w1,
                     jnp.where(lane == 3, w2, 0.0))))


def _router(x2, norm_moe, wr, br, groups, per_group, tm):
    t, d = x2.shape
    row = lambda i: (i, 0)
    const = lambda i: (0, 0)
    kern = functools.partial(_router_kernel, groups=groups, per_group=per_group)
    return pl.pallas_call(
        kern,
        grid=(t // tm,),
        in_specs=[pl.BlockSpec((tm, d), row), pl.BlockSpec((1, d), const),
                  pl.BlockSpec((d, LANES), const), pl.BlockSpec((1, LANES), const)],
        out_specs=pl.BlockSpec((tm, LANES), row),
        out_shape=jax.ShapeDtypeStruct((t, LANES), F32),
        compiler_params=_cparams("parallel"),
    )(x2, norm_moe.reshape(1, d), wr, br)


def _expert_kernel(te_ref, tv_ref, gcur_ref, gnext_ref, sidx_ref, xm_ref, g_ref, wg_ref, wu_ref,
                   wd_ref, y_ref, xbuf, ybuf, wgb, wub, wdb, gsem, ssem):
    i = pl.program_id(0)
    nt = pl.num_programs(0)
    slot = i % 2
    groups8 = xbuf.shape[1]
    rows = 8 * groups8
    d = xbuf.shape[3]
    nxt = jnp.minimum(i + 1, nt - 1)
    valid = tv_ref[i] > 0
    has_next = (i + 1 < nt) & (tv_ref[nxt] > 0)

    def gather_copy(src_row, s, g8, u):
        return pltpu.make_async_copy(xm_ref.at[pl.ds(src_row, 1), :],
                                     xbuf.at[s, g8, pl.ds(u, 1), :], gsem.at[s])

    def scatter_copy(s, g8, u, dst_row):
        return pltpu.make_async_copy(ybuf.at[s, g8, pl.ds(u, 1), :],
                                     y_ref.at[pl.ds(dst_row, 1), :], ssem.at[s])

    def per_row(fn):
        def g8_body(g8, carry):
            for u in range(8):
                fn(g8, u)
            return carry
        lax.fori_loop(0, groups8, g8_body, 0)

    def gather_start(idx_ref, s):
        per_row(lambda g8, u: gather_copy(idx_ref[0, g8 * 8 + u], s, g8, u).start(priority=u % 2))

    def gather_wait(s):
        pltpu.make_async_copy(xbuf.at[s], xbuf.at[s], gsem.at[s]).wait()

    def scatter_start(s):
        per_row(lambda g8, u: scatter_copy(s, g8, u, sidx_ref[0, g8 * 8 + u]).start(priority=u % 2))

    def scatter_wait(s):
        pltpu.make_async_copy(ybuf.at[s], ybuf.at[s], ssem.at[s]).wait()

    @pl.when(i == 0)
    def _():
        ybuf[1] = jnp.zeros(ybuf.shape[1:], ybuf.dtype)
        n_real = y_ref.shape[0] - 2 * rows

        def fill_copy(g):
            return pltpu.make_async_copy(ybuf.at[1, g % groups8],
                                         y_ref.at[pl.ds(n_real + g * 8, 8), :], ssem.at[1])

        def fill_start(g, carry):
            fill_copy(g).start()
            return carry

        def fill_wait(g, carry):
            fill_copy(g).wait()
            return carry

        lax.fori_loop(0, 2 * groups8, fill_start, 0)
        lax.fori_loop(0, 2 * groups8, fill_wait, 0)

    @pl.when((i == 0) & valid)
    def _():
        gather_start(gcur_ref, 0)

    @pl.when(valid & ((i == 0) | (te_ref[i] != te_ref[jnp.maximum(i - 1, 0)])))
    def _():
        wgb[...] = wg_ref[...].astype(BF16)
        wub[...] = wu_ref[...].astype(BF16)
        wdb[...] = wd_ref[...].astype(BF16)

    @pl.when(valid)
    def _():
        @pl.when(has_next)
        def _():
            gather_start(gnext_ref, 1 - slot)

        gather_wait(slot)

        @pl.when(i >= 2)
        def _():
            scatter_wait(slot)

        x = xbuf[slot].reshape(rows, d)
        xb = (x * lax.rsqrt(jnp.mean(x * x, axis=-1, keepdims=True) + EPS) * g_ref[...]).astype(BF16)
        gate = _dot(xb, wgb[...])
        hid = gate * _sigmoid(gate) * _dot(xb, wub[...])
        ybuf[slot] = _dot(hid.astype(BF16), wdb[...]).reshape(groups8, 8, d)
        scatter_start(slot)

        @pl.when(jnp.logical_not(has_next))
        def _():
            scatter_wait(slot)

            @pl.when(i >= 1)
            def _():
                scatter_wait(1 - slot)


def _experts(xm, gain, tile_expert, tile_valid, gidx, sidx, wg, wu, wd, n_out):
    t, d = xm.shape
    tm = EXPERT_TILE
    nt = tile_expert.shape[0]
    hid = wg.shape[2]
    smem_rows = lambda imap: pl.BlockSpec((None, 1, tm), imap, memory_space=pltpu.SMEM)
    wspec = lambda shape: pl.BlockSpec((None,) + shape, lambda i, te, tv: (te[i], 0, 0))
    grid_spec = pltpu.PrefetchScalarGridSpec(
        num_scalar_prefetch=2,
        grid=(nt,),
        in_specs=[smem_rows(lambda i, te, tv: (i, 0, 0)),
                  smem_rows(lambda i, te, tv: (jnp.minimum(i + 1, nt - 1), 0, 0)),
                  smem_rows(lambda i, te, tv: (i, 0, 0)),
                  pl.BlockSpec(memory_space=pl.ANY),
                  pl.BlockSpec((1, d), lambda i, te, tv: (0, 0)),
                  wspec((d, hid)), wspec((d, hid)), wspec((hid, d))],
        out_specs=pl.BlockSpec(memory_space=pl.ANY),
        scratch_shapes=[pltpu.VMEM((2, tm // 8, 8, d), F32), pltpu.VMEM((2, tm // 8, 8, d), F32),
                        pltpu.VMEM((d, hid), BF16), pltpu.VMEM((d, hid), BF16),
                        pltpu.VMEM((hid, d), BF16),
                        pltpu.SemaphoreType.DMA((2,)), pltpu.SemaphoreType.DMA((2,))],
    )
    g3 = gidx.reshape(nt, 1, tm)
    return pl.pallas_call(
        _expert_kernel,
        grid_spec=grid_spec,
        out_shape=jax.ShapeDtypeStruct((n_out, d), F32),
        compiler_params=_cparams("arbitrary"),
    )(tile_expert, tile_valid, g3, g3, sidx.reshape(nt, 1, tm), xm, gain.reshape(1, d), wg, wu, wd)


def _combine_kernel(x_ref, y1_ref, y2_ref, route_ref, g_ref, o_ref):
    w1 = route_ref[:, 2:3]
    w2 = route_ref[:, 3:4]
    x = x_ref[...] + w1 * y1_ref[...] + w2 * y2_ref[...]
    o_ref[...] = x * lax.rsqrt(jnp.mean(x * x, axis=-1, keepdims=True) + EPS) * g_ref[...]


def _combine(x2, y, route, norm_final, tm):
    t, d = x2.shape
    row = lambda i: (i, 0)
    return pl.pallas_call(
        _combine_kernel,
        grid=(t // tm,),
        in_specs=[pl.BlockSpec((tm, d), row), pl.BlockSpec((tm, d), row),
                  pl.BlockSpec((tm, d), lambda i: (i + t // tm, 0)),
                  pl.BlockSpec((tm, LANES), row), pl.BlockSpec((1, d), lambda i: (0, 0))],
        out_specs=pl.BlockSpec((tm, d), row),
        out_shape=jax.ShapeDtypeStruct((t, d), F32),
        compiler_params=_cparams("parallel"),
    )(x2, y, y, route, norm_final.reshape(1, d))


def _moe_plan(route, n_experts, tile):
    t = route.shape[0]
    pairs = MOE_TOPK * t
    n_tiles = pairs // tile + n_experts
    e_flat = route[:, :MOE_TOPK].astype(I32).reshape(pairs)
    order = jnp.argsort(e_flat, stable=True).astype(I32)
    counts = jnp.sum((e_flat[:, None] == jnp.arange(n_experts, dtype=I32)[None, :]).astype(I32), axis=0)
    tiles_per = (counts + tile - 1) // tile
    tile_end = jnp.cumsum(tiles_per)
    sorted_start = jnp.cumsum(counts) - counts
    tile_ids = jnp.arange(n_tiles, dtype=I32)
    tile_valid = (tile_ids < tile_end[-1]).astype(I32)
    tile_expert = jnp.minimum(jnp.sum((tile_ids[:, None] >= tile_end[None, :]).astype(I32), axis=1),
                              n_experts - 1)
    last_expert = tile_expert[jnp.maximum(tile_end[-1] - 1, 0)]
    tile_expert = jnp.where(tile_valid > 0, tile_expert, last_expert)
    tile_first = (tile_end - tiles_per)[tile_expert]
    lane_row = jnp.arange(tile, dtype=I32)[None, :]
    row_in_group = ((tile_ids - tile_first) * tile)[:, None] + lane_row
    row_valid = (row_in_group < counts[tile_expert][:, None]) & (tile_valid[:, None] > 0)
    src = jnp.clip(sorted_start[tile_expert][:, None] + row_in_group, 0, pairs - 1)
    pair = order[src]
    token = jnp.where(row_valid, pair // MOE_TOPK, 0)
    spare = pairs + (tile_ids % 2)[:, None] * tile + lane_row
    dest = jnp.where(row_valid, (pair % MOE_TOPK) * t + pair // MOE_TOPK, spare)
    return tile_expert, tile_valid, token.reshape(-1), dest.reshape(-1)


def kernel(x, mem, norm_mix, w_in, conv_w, conv_b, dt_bias, a_log, d_skip, ssm_norm, w_ssm_out,
           w_att_out, w_mix_out, norm_x, norm_mem, w_xq, w_xkv, w_xo, norm_moe, w_rg, b_rg, w_re,
           b_re, w_e_gate, w_e_up, w_e_down, norm_final):
    batch, seq, d = x.shape
    mem_len = mem.shape[1]
    t = batch * seq
    heads = dt_bias.shape[1]
    d_inner = heads * SSM_HEAD_DIM
    conv_dim = conv_w.shape[1]
    att_w = w_att_out.shape[1]
    att_heads = att_w // ATT_HEAD_DIM
    n_experts = w_re.shape[2]
    groups = w_rg.shape[2]
    per_group = n_experts // groups
    idx_heads = (w_in.shape[2] - (d_inner + conv_dim + heads + att_w + 2 * ATT_HEAD_DIM
                                  + IDX_HEAD_DIM + 2 * d)) // (IDX_HEAD_DIM + 1)
    idx_w = idx_heads * IDX_HEAD_DIM
    assert w_in.shape[0] == 1, "one layer; the final norm is fused into the layer's last call"
    assert seq % SSM_CHUNK == 0 and seq % Q_BLOCK == 0 and heads + idx_heads <= LANES
    assert groups + n_experts <= LANES and t % 1024 == 0 and heads % (2 * SSM_GROUPS) == 0
    li = 0

    sizes = (d_inner, conv_dim, heads, att_w, ATT_HEAD_DIM, ATT_HEAD_DIM, idx_w, IDX_HEAD_DIM,
             idx_heads, d, d)
    offs = [0]
    for s in sizes:
        offs.append(offs[-1] + s)
    wi_ = w_in[li]
    col = lambda k: wi_[:, offs[k]:offs[k + 1]]
    pad = lambda a, n: jnp.pad(a, ((0, 0), (0, n - a.shape[1])))
    lpad = lambda a, n: jnp.pad(a, ((0, 0), (n - a.shape[1], 0)))
    both = lambda a: [pad(a, LANES), lpad(a, LANES)]
    w_z = col(0).astype(BF16)
    w_xbc = col(1).astype(BF16)
    w_small = pad(jnp.concatenate([col(2), col(8)], axis=1), LANES).astype(BF16)
    w_qq = jnp.concatenate([col(3) * (LOG2E * ATT_HEAD_DIM ** -0.5), col(6)], axis=1).astype(BF16)
    w_kv = jnp.concatenate(both(col(7)) + [pad(col(4), LANES), pad(col(5), LANES)],
                           axis=1).astype(BF16)
    w_gates = jnp.concatenate([col(9), col(10)], axis=1).astype(BF16)

    h = x.reshape(t, d)
    u = _rmsnorm(h, norm_mix[li], BF16, 512)
    z = _matmul(u, w_z, BF16, 1024, 1024)
    xbc = _matmul(u, w_xbc, BF16, 1024, 1024)
    small = _matmul(u, w_small, F32, 512, LANES)
    qq = _matmul(u, w_qq, BF16, 512, att_w + idx_w)
    kvp = _matmul(u, w_kv, BF16, 512, 4 * LANES)
    gates = _matmul(u, w_gates, BF16, 1024, 1024)

    dtt = jnp.swapaxes(small[:, :heads].reshape(batch, seq, heads), 1, 2)
    ys = _ssd(z, xbc, small, dtt, conv_w[li], conv_b[li], dt_bias[li], a_log[li], d_skip[li],
              ssm_norm[li], batch, seq)
    wit = jnp.swapaxes(small[:, heads:heads + idx_heads].reshape(batch, seq, idx_heads), 1, 2)
    oa = _dsa(qq, kvp, wit, batch, seq, att_heads, idx_heads)

    x1, xn1 = _merge(h, ys, oa, gates, w_ssm_out[li].astype(BF16), w_att_out[li].astype(BF16),
                     w_mix_out[li].astype(BF16), norm_x[li], 512)

    mn = _rmsnorm(mem.reshape(batch * mem_len, d), norm_mem[li], BF16, 512)
    kvm = _matmul(mn, w_xkv[li].astype(BF16), BF16, 512, 1024)
    x2 = _xattn(x1, xn1, kvm, w_xq[li].astype(BF16), w_xo[li].astype(BF16), batch, seq,
                mem_len, 512)

    wr = pad(jnp.concatenate([w_rg[li], w_re[li]], axis=1), LANES)
    br = pad(jnp.concatenate([b_rg[li], b_re[li]]).reshape(1, -1), LANES)
    route = _router(x2, norm_moe[li], wr, br, groups, per_group, 512)

    tile_expert, tile_valid, token, dest = _moe_plan(route, n_experts, EXPERT_TILE)
    ye = _experts(x2, norm_moe[li], tile_expert, tile_valid, token, dest, w_e_gate[li], w_e_up[li],
                  w_e_down[li], MOE_TOPK * t + 2 * EXPERT_TILE)
    out = _combine(x2, ye, route, norm_final, 512)
    return out.reshape(batch, seq, d)
```
